```python
import math
import jax, jax.numpy as jnp
from jax import lax
import numpy as np

D_MODEL = 1024
BATCH = 4
SEQ = 8192
DEPTH = 2
DEC_BATCH = 32
DEC_SEQ = 1
PAST_LEN = 16384
PAGE_SIZE = 128

HEAD_DIM = 64
H_SB = 4
H_MB = 8
SSM_GROUPS = 16
SSM_GROUP_CH = 16
SSM_STATE = 64
D_SB = H_SB * HEAD_DIM
D_MB = H_MB * HEAD_DIM
D_SSM = SSM_GROUPS * SSM_GROUP_CH
N_BRANCH = 3
D_FF = 2816
MOBA_BLOCK = 256
MOBA_TOPK = 3
SB_Q_BLOCK = 128
MOBA_Q_BLOCK = 32
ALPHA = (2 * DEPTH) ** 0.25
BETA_INIT = (8 * DEPTH) ** -0.25
LN_EPS = 1e-5
DT_MIN = 1e-3
DT_MAX = 1e-1
_SIZES = (D_SB, D_SB, D_SB, D_MB, D_MB, D_MB, D_SSM, D_MODEL, D_MODEL, D_MODEL)
_COL_SCALE = (1.0, 1.0, BETA_INIT, 1.0, 1.0, BETA_INIT, 1.0, 1.0, 1.0, 1.0)
D_IN = sum(_SIZES)
SPLIT_AT = tuple(int(s) for s in np.cumsum(_SIZES)[:-1])

kernel_name = 'hybrid_sb_moba_s5_decoder_step'


def layer_norm(x, g, b):
    xf = x.astype(jnp.float32)
    mu = xf.mean(-1, keepdims=True)
    var = jnp.square(xf - mu).mean(-1, keepdims=True)
    return ((xf - mu) * lax.rsqrt(var + LN_EPS)).astype(x.dtype) * g + b


def swiglu(x, w_up, w_down):
    gate, up = jnp.split(x @ w_up, 2, axis=-1)
    return (jax.nn.silu(gate) * up) @ w_down


def sweep_queries(attend, q, q_pos, block):
    b, h, t, d = q.shape
    if t <= block or t % block:
        return attend(q, q_pos)
    n = t // block
    qb = jnp.moveaxis(q.reshape(b, h, n, block, d), 2, 0)
    pb = q_pos.reshape(n, block)
    out = lax.map(lambda qp: attend(qp[0], qp[1]), (qb, pb))
    return jnp.moveaxis(out, 0, 2).reshape(b, h, t, out.shape[-1])


def stick_breaking(q, q_pos, k, v, k_pos):
    z = jnp.einsum('bhqd,bhkd->bhqk', q, k).astype(jnp.float32) * (HEAD_DIM ** -0.5)
    mask = k_pos[None, :] < q_pos[:, None]
    log_keep = jnp.where(mask, jax.nn.log_sigmoid(-z), 0.0)
    between = lax.cumsum(log_keep, axis=3, reverse=True) - log_keep
    w = jnp.where(mask, jnp.exp(jax.nn.log_sigmoid(z) + between), 0.0)
    return jnp.einsum('bhqk,bhkd->bhqd', w.astype(v.dtype), v)


def moba_blocks(k, v):
    b, h, l, d = k.shape
    nb = -(-l // MOBA_BLOCK)
    pad = ((0, 0), (0, 0), (0, nb * MOBA_BLOCK - l), (0, 0))
    kb = jnp.pad(k, pad).reshape(b, h, nb, MOBA_BLOCK, d)
    vb = jnp.pad(v, pad).reshape(b, h, nb, MOBA_BLOCK, d)
    kmean = kb.astype(jnp.float32).mean(axis=3)
    return kb, vb, kmean


def moba_attend(q, q_pos, kb, vb, kmean):
    nb = kb.shape[2]
    own = q_pos // MOBA_BLOCK
    score = jnp.einsum('bhqd,bhnd->bhqn', q.astype(jnp.float32), kmean)
    past_ok = jnp.arange(nb)[None, :] < own[:, None]
    score = jnp.where(past_ok, score, -jnp.inf)
    top_score, top_idx = lax.top_k(score, min(MOBA_TOPK, nb))
    own_idx = jnp.broadcast_to(own[:, None], top_idx.shape[:-1] + (1,)).astype(top_idx.dtype)
    idx = jnp.concatenate([top_idx, own_idx], axis=-1)
    blk_ok = jnp.concatenate([jnp.isfinite(top_score), jnp.ones(own_idx.shape, bool)], axis=-1)
    gather = jax.vmap(jax.vmap(lambda blocks, ix: jnp.take(blocks, ix, axis=0)))
    kg = gather(kb, idx)
    vg = gather(vb, idx)
    logits = jnp.einsum('bhqd,bhqnkd->bhqnk', q, kg).astype(jnp.float32) * (HEAD_DIM ** -0.5)
    k_pos = idx[..., None] * MOBA_BLOCK + jnp.arange(MOBA_BLOCK)
    mask = blk_ok[..., None] & (k_pos <= q_pos[:, None, None])
    logits = jnp.where(mask, logits, -jnp.inf)
    p = jax.nn.softmax(logits.reshape(logits.shape[:3] + (-1,)), axis=-1).reshape(logits.shape)
    return jnp.einsum('bhqnk,bhqnkd->bhqd', p.astype(vg.dtype), vg)


def _complex_affine_combine(e1, e2):
    ar1, ai1, br1, bi1 = e1
    ar2, ai2, br2, bi2 = e2
    return (ar2 * ar1 - ai2 * ai1, ar2 * ai1 + ai2 * ar1,
            ar2 * br1 - ai2 * bi1 + br2, ar2 * bi1 + ai2 * br1 + bi2)


def s5_mixer(u, h0_re, h0_im, lw):
    f32 = jnp.float32
    bsz, t, _ = u.shape
    uf = u.astype(f32).reshape(bsz, t, SSM_GROUPS, SSM_GROUP_CH)
    a_re, a_im = lw['ssm_a_re'].astype(f32), lw['ssm_a_im'].astype(f32)
    dt = jnp.exp(lw['ssm_log_dt'].astype(f32))[:, None]
    mag = jnp.exp(a_re * dt)
    lam_re, lam_im = mag * jnp.cos(a_im * dt), mag * jnp.sin(a_im * dt)
    den = a_re * a_re + a_im * a_im
    num_re, num_im = lam_re - 1.0, lam_im
    coef_re = (num_re * a_re + num_im * a_im) / den
    coef_im = (num_im * a_re - num_re * a_im) / den
    b_re, b_im = lw['ssm_b_re'].astype(f32), lw['ssm_b_im'].astype(f32)
    bb_re = coef_re[..., None] * b_re - coef_im[..., None] * b_im
    bb_im = coef_re[..., None] * b_im + coef_im[..., None] * b_re
    bu_re = jnp.einsum('btgc,gpc->btgp', uf, bb_re)
    bu_im = jnp.einsum('btgc,gpc->btgp', uf, bb_im)
    h0r, h0i = h0_re.astype(f32), h0_im.astype(f32)
    bu_re = bu_re.at[:, 0].add(lam_re * h0r - lam_im * h0i)
    bu_im = bu_im.at[:, 0].add(lam_re * h0i + lam_im * h0r)
    lr = jnp.broadcast_to(lam_re, bu_re.shape)
    li = jnp.broadcast_to(lam_im, bu_im.shape)
    _, _, h_re, h_im = lax.associative_scan(_complex_affine_combine, (lr, li, bu_re, bu_im), axis=1)
    c_re, c_im = lw['ssm_c_re'].astype(f32), lw['ssm_c_im'].astype(f32)
    y = jnp.einsum('btgp,gcp->btgc', h_re, c_re) - jnp.einsum('btgp,gcp->btgc', h_im, c_im)
    y = y.reshape(bsz, t, D_SSM) + lw['ssm_d'].astype(f32) * uf.reshape(bsz, t, D_SSM)
    y = y.astype(u.dtype)
    val, gate = jnp.split(y @ lw['w_glu'], 2, axis=-1)
    return val * jax.nn.sigmoid(gate), h_re[:, -1].astype(h0_re.dtype), h_im[:, -1].astype(h0_im.dtype)


def token_mixer(x, past, lw):
    past_k_sb, past_v_sb, past_k_mb, past_v_mb, h0_re, h0_im = past
    bsz, t, _ = x.shape
    p0 = past_k_sb.shape[1]
    (q_sb, k_sb, v_sb, q_mb, k_mb, v_mb, u, g_sb, g_mb, g_ssm) = jnp.split(
        x @ lw['w_in'] + lw['b_in'], SPLIT_AT, axis=-1)
    heads = lambda a, n: a.reshape(bsz, t, n, HEAD_DIM)
    to_bhsd = lambda a: jnp.swapaxes(a, 1, 2)
    k_sb, v_sb = heads(k_sb, H_SB), heads(v_sb, H_SB)
    k_mb, v_mb = heads(k_mb, H_MB), heads(v_mb, H_MB)
    q_pos = p0 + jnp.arange(t, dtype=jnp.int32)
    k_pos = jnp.arange(p0 + t, dtype=jnp.int32)
    ksb_all = to_bhsd(jnp.concatenate([past_k_sb, k_sb], axis=1))
    vsb_all = to_bhsd(jnp.concatenate([past_v_sb, v_sb], axis=1))
    o_sb = sweep_queries(lambda qq, pp: stick_breaking(qq, pp, ksb_all, vsb_all, k_pos),
                         to_bhsd(heads(q_sb, H_SB)), q_pos, SB_Q_BLOCK)
    kb, vb, kmean = moba_blocks(to_bhsd(jnp.concatenate([past_k_mb, k_mb], axis=1)),
                                to_bhsd(jnp.concatenate([past_v_mb, v_mb], axis=1)))
    o_mb = sweep_queries(lambda qq, pp: moba_attend(qq, pp, kb, vb, kmean),
                         to_bhsd(heads(q_mb, H_MB)), q_pos, MOBA_Q_BLOCK)
    o_sb = to_bhsd(o_sb).reshape(bsz, t, D_SB)
    o_mb = to_bhsd(o_mb).reshape(bsz, t, D_MB)
    o_ssm, h_re, h_im = s5_mixer(u, h0_re, h0_im, lw)
    merged = (jax.nn.sigmoid(g_sb) * (o_sb @ lw['w_br_sb'])
              + jax.nn.sigmoid(g_mb) * (o_mb @ lw['w_br_mb'])
              + jax.nn.sigmoid(g_ssm) * (o_ssm @ lw['w_br_ssm']))
    return merged @ lw['w_out'], (k_sb, v_sb, k_mb, v_mb, h_re, h_im)


def trunk_layer(x, past, lw):
    x = layer_norm(ALPHA * x + 0.5 * swiglu(x, lw['w_ffn1_up'], lw['w_ffn1_down']), lw['ln1_g'], lw['ln1_b'])
    mix, new_state = token_mixer(x, past, lw)
    x = layer_norm(ALPHA * x + mix, lw['ln2_g'], lw['ln2_b'])
    x = layer_norm(ALPHA * x + 0.5 * swiglu(x, lw['w_ffn2_up'], lw['w_ffn2_down']), lw['ln3_g'], lw['ln3_b'])
    return x, new_state


def stack_layers(states):
    return tuple(jnp.stack(group) for group in zip(*states))


def setup_inputs(seed: int = 0) -> dict:
    key = jax.random.key(seed)
    ks = iter(jax.random.split(key, 48))
    nrm = lambda shape, scale: jax.random.normal(next(ks), shape, jnp.float32) * scale
    n_pages = PAST_LEN // PAGE_SIZE
    n_pool = (5 * DEC_BATCH * n_pages) // 4
    col_scale = jnp.concatenate([jnp.full((s,), c, jnp.float32) for s, c in zip(_SIZES, _COL_SCALE)])
    inp = {}
    inp['x_prompt'] = nrm((BATCH, SEQ, D_MODEL), 1.0)
    inp['x_sample'] = nrm((DEC_BATCH, DEC_SEQ, D_MODEL), 1.0)
    inp['cache_k_sb'] = nrm((DEPTH, n_pool, PAGE_SIZE, H_SB, HEAD_DIM), 1.0)
    inp['cache_v_sb'] = nrm((DEPTH, n_pool, PAGE_SIZE, H_SB, HEAD_DIM), 1.0)
    inp['cache_k_mb'] = nrm((DEPTH, n_pool, PAGE_SIZE, H_MB, HEAD_DIM), 1.0)
    inp['cache_v_mb'] = nrm((DEPTH, n_pool, PAGE_SIZE, H_MB, HEAD_DIM), 1.0)
    inp['state_ssm_re'] = nrm((DEPTH, DEC_BATCH, SSM_GROUPS, SSM_STATE), 0.1)
    inp['state_ssm_im'] = nrm((DEPTH, DEC_BATCH, SSM_GROUPS, SSM_STATE), 0.1)
    inp['page_table'] = jax.random.permutation(next(ks), n_pool)[:DEC_BATCH * n_pages].reshape(
        DEC_BATCH, n_pages).astype(jnp.int32)
    inp['ln1_g'] = 1.0 + nrm((DEPTH, D_MODEL), 0.02)
    inp['ln1_b'] = nrm((DEPTH, D_MODEL), 0.02)
    inp['w_ffn1_up'] = nrm((DEPTH, D_MODEL, 2 * D_FF), D_MODEL ** -0.5)
    inp['w_ffn1_down'] = nrm((DEPTH, D_FF, D_MODEL), BETA_INIT * D_FF ** -0.5)
    inp['w_in'] = nrm((DEPTH, D_MODEL, D_IN), D_MODEL ** -0.5) * col_scale
    inp['b_in'] = nrm((DEPTH, D_IN), 0.01)
    inp['ssm_a_re'] = -0.5 + nrm((DEPTH, SSM_GROUPS, SSM_STATE), 0.01)
    inp['ssm_a_im'] = math.pi * jnp.arange(SSM_STATE, dtype=jnp.float32) + nrm((DEPTH, SSM_GROUPS, SSM_STATE), 0.01)
    inp['ssm_log_dt'] = jax.random.uniform(next(ks), (DEPTH, SSM_GROUPS), jnp.float32,
                                           minval=math.log(DT_MIN), maxval=math.log(DT_MAX))
    inp['ssm_b_re'] = nrm((DEPTH, SSM_GROUPS, SSM_STATE, SSM_GROUP_CH), (2 * SSM_GROUP_CH) ** -0.5)
    inp['ssm_b_im'] = nrm((DEPTH, SSM_GROUPS, SSM_STATE, SSM_GROUP_CH), (2 * SSM_GROUP_CH) ** -0.5)
    inp['ssm_c_re'] = nrm((DEPTH, SSM_GROUPS, SSM_GROUP_CH, SSM_STATE), SSM_STATE ** -0.5)
    inp['ssm_c_im'] = nrm((DEPTH, SSM_GROUPS, SSM_GROUP_CH, SSM_STATE), SSM_STATE ** -0.5)
    inp['ssm_d'] = nrm((DEPTH, D_SSM), 1.0)
    inp['w_glu'] = nrm((DEPTH, D_SSM, 2 * D_SSM), D_SSM ** -0.5)
    inp['w_br_sb'] = nrm((DEPTH, D_SB, D_MODEL), BETA_INIT * D_SB ** -0.5)
    inp['w_br_mb'] = nrm((DEPTH, D_MB, D_MODEL), BETA_INIT * D_MB ** -0.5)
    inp['w_br_ssm'] = nrm((DEPTH, D_SSM, D_MODEL), BETA_INIT * D_SSM ** -0.5)
    inp['w_out'] = nrm((DEPTH, D_MODEL, D_MODEL), BETA_INIT * D_MODEL ** -0.5)
    inp['ln2_g'] = 1.0 + nrm((DEPTH, D_MODEL), 0.02)
    inp['ln2_b'] = nrm((DEPTH, D_MODEL), 0.02)
    inp['w_ffn2_up'] = nrm((DEPTH, D_MODEL, 2 * D_FF), D_MODEL ** -0.5)
    inp['w_ffn2_down'] = nrm((DEPTH, D_FF, D_MODEL), BETA_INIT * D_FF ** -0.5)
    inp['ln3_g'] = 1.0 + nrm((DEPTH, D_MODEL), 0.02)
    inp['ln3_b'] = nrm((DEPTH, D_MODEL), 0.02)
    return inp


def reference(x_prompt, x_sample, cache_k_sb, cache_v_sb, cache_k_mb, cache_v_mb,
              state_ssm_re, state_ssm_im, page_table,
              ln1_g, ln1_b, w_ffn1_up, w_ffn1_down, w_in, b_in,
              ssm_a_re, ssm_a_im, ssm_log_dt, ssm_b_re, ssm_b_im, ssm_c_re, ssm_c_im, ssm_d,
              w_glu, w_br_sb, w_br_mb, w_br_ssm, w_out, ln2_g, ln2_b,
              w_ffn2_up, w_ffn2_down, ln3_g, ln3_b):
    weights = dict(ln1_g=ln1_g, ln1_b=ln1_b, w_ffn1_up=w_ffn1_up, w_ffn1_down=w_ffn1_down,
                   w_in=w_in, b_in=b_in, ssm_a_re=ssm_a_re, ssm_a_im=ssm_a_im,
                   ssm_log_dt=ssm_log_dt, ssm_b_re=ssm_b_re, ssm_b_im=ssm_b_im,
                   ssm_c_re=ssm_c_re, ssm_c_im=ssm_c_im, ssm_d=ssm_d, w_glu=w_glu,
                   w_br_sb=w_br_sb, w_br_mb=w_br_mb, w_br_ssm=w_br_ssm, w_out=w_out,
                   ln2_g=ln2_g, ln2_b=ln2_b, w_ffn2_up=w_ffn2_up, w_ffn2_down=w_ffn2_down,
                   ln3_g=ln3_g, ln3_b=ln3_b)
    bsz = x_prompt.shape[0]
    dbsz, n_pages = page_table.shape

    def empty_rows(n_heads):
        return jnp.zeros((bsz, 0, n_heads, HEAD_DIM), x_prompt.dtype)

    h_zero = jnp.zeros((bsz, SSM_GROUPS, SSM_STATE), state_ssm_re.dtype)

    def paged_rows(cache, l):
        rows = cache[l, page_table]
        return rows.reshape((dbsz, n_pages * rows.shape[2]) + rows.shape[3:])

    y_p, y_s = x_prompt, x_sample
    prompt_states, sample_states = [], []
    for l in range(DEPTH):
        lw = {name: w[l] for name, w in weights.items()}
        y_p, st_p = trunk_layer(y_p, (empty_rows(H_SB), empty_rows(H_SB), empty_rows(H_MB),
                                      empty_rows(H_MB), h_zero, h_zero), lw)
        prompt_states.append(st_p)
        past = (paged_rows(cache_k_sb, l), paged_rows(cache_v_sb, l),
                paged_rows(cache_k_mb, l), paged_rows(cache_v_mb, l),
                state_ssm_re[l], state_ssm_im[l])
        y_s, st_s = trunk_layer(y_s, past, lw)
        sample_states.append(st_s)
    k_sb_p, v_sb_p, k_mb_p, v_mb_p, h_re_p, h_im_p = stack_layers(prompt_states)
    k_sb_s, v_sb_s, k_mb_s, v_mb_s, h_re_s, h_im_s = stack_layers(sample_states)
    return (y_p, y_s, k_sb_p, v_sb_p, k_mb_p, v_mb_p, h_re_p, h_im_p,
            k_sb_s, v_sb_s, k_mb_s, v_mb_s, h_re_s, h_im_s)
```

```python
import functools
import math

import jax
import jax.numpy as jnp
from jax import lax
from jax.experimental import pallas as pl
from jax.experimental.pallas import tpu as pltpu

F32 = jnp.float32
BF16 = jnp.bfloat16

HEAD_DIM = 64
SSM_GROUP_CH = 16
MOBA_BLOCK = 256
MOBA_TOPK = 3
LN_EPS = 1e-5
LANES = 128
SUBLANES = 8
HEADS_PER_VREG = LANES // HEAD_DIM
SSM_CHUNK = 16
NEG_BIG = -1e30
F32_EXP_ZERO = -104.0
VMEM_LIMIT = 56 * 1024 * 1024

_HI = lax.Precision.HIGHEST


def _cparams(sem):
    return pltpu.CompilerParams(dimension_semantics=sem, vmem_limit_bytes=VMEM_LIMIT)


def _resident(shape):
    return pl.BlockSpec(shape, lambda *_: (0,) * len(shape), pipeline_mode=pl.Buffered(1))


def _dot(a, b):
    return jnp.dot(a, b, preferred_element_type=F32)


def _dot_nt(a, b):
    return lax.dot_general(a, b, (((1,), (1,)), ((), ())), preferred_element_type=F32)


def _layer_norm(y, g, b):
    mu = jnp.mean(y, axis=-1, keepdims=True)
    d = y - mu
    var = jnp.mean(d * d, axis=-1, keepdims=True)
    return d * lax.rsqrt(var + LN_EPS) * g + b


def _neg_softplus(z):
    return -(jnp.maximum(z, 0.0) + jnp.log(1.0 + jnp.exp(-jnp.abs(z))))


def _split_bf16(x):
    hi = x.astype(BF16)
    lo = (x - hi.astype(F32)).astype(BF16)
    return hi, lo


def _ffn_ln_kernel(x_ref, wup_ref, wdn_ref, g_ref, b_ref, o_ref, *, alpha, d_ff, n_chunk):
    x = x_ref[...]
    xb = x.astype(BF16)
    fc = d_ff // n_chunk
    acc = jnp.zeros_like(x)
    for c in range(n_chunk):
        a = c * fc
        gate = _dot(xb, wup_ref[:, a:a + fc])
        up = _dot(xb, wup_ref[:, d_ff + a:d_ff + a + fc])
        h = (gate * jax.nn.sigmoid(gate) * up).astype(BF16)
        acc = acc + _dot(h, wdn_ref[a:a + fc, :])
    o_ref[...] = _layer_norm(alpha * x + 0.5 * acc, g_ref[...], b_ref[...])


def _ffn_ln(x, w_up, w_down, g, b, alpha, tm):
    n, d = x.shape
    d_ff = w_down.shape[0]
    n_chunk = 2 if d_ff % (2 * LANES) == 0 else 1
    return pl.pallas_call(
        functools.partial(_ffn_ln_kernel, alpha=alpha, d_ff=d_ff, n_chunk=n_chunk),
        grid=(n // tm,),
        in_specs=[pl.BlockSpec((tm, d), lambda i: (i, 0)),
                  _resident((d, 2 * d_ff)), _resident((d_ff, d)),
                  _resident((1, d)), _resident((1, d))],
        out_specs=pl.BlockSpec((tm, d), lambda i: (i, 0)),
        out_shape=jax.ShapeDtypeStruct((n, d), F32),
        compiler_params=_cparams(("parallel",)),
        name="ffn_ln",
    )(x, w_up, w_down, g, b)


def _in_proj_kernel(x_ref, w_ref, b_ref, ksb_ref, vsb_ref, kmb_ref, vmb_ref,
                    sb16_ref, mb16_ref, qmb_ref, u_ref, *maybe_ksum_ref, d_sb, d_mb, d_ssm):
    r = _dot(x_ref[...].astype(BF16), w_ref[...]) + b_ref[...]
    o = 0
    q_sb = r[:, o:o + d_sb]; o += d_sb
    k_sb = r[:, o:o + d_sb]; o += d_sb
    v_sb = r[:, o:o + d_sb]; o += d_sb
    q_mb = r[:, o:o + d_mb]; o += d_mb
    k_mb = r[:, o:o + d_mb]; o += d_mb
    v_mb = r[:, o:o + d_mb]; o += d_mb
    u = r[:, o:o + d_ssm]
    scale = HEAD_DIM ** -0.5
    ksb_ref[...] = k_sb
    vsb_ref[...] = v_sb
    kmb_ref[...] = k_mb
    vmb_ref[...] = v_mb
    sb16_ref[:, 0:d_sb] = (q_sb * scale).astype(BF16)
    sb16_ref[:, d_sb:2 * d_sb] = k_sb.astype(BF16)
    sb16_ref[:, 2 * d_sb:3 * d_sb] = v_sb.astype(BF16)
    mb16_ref[:, 0:d_mb] = (q_mb * scale).astype(BF16)
    mb16_ref[:, d_mb:2 * d_mb] = k_mb.astype(BF16)
    mb16_ref[:, 2 * d_mb:3 * d_mb] = v_mb.astype(BF16)
    qmb_ref[...] = q_mb
    u_ref[...] = u
    if maybe_ksum_ref:
        ksum_ref, = maybe_ksum_ref
        tm = k_mb.shape[0]
        for j in range(tm // MOBA_BLOCK):
            ksum_ref[0, j:j + 1, :] = jnp.sum(
                k_mb[j * MOBA_BLOCK:(j + 1) * MOBA_BLOCK, :], axis=0, keepdims=True)


def _in_proj(x, w, b, d_sb, d_mb, d_ssm, tm, with_ksum):
    n, d = x.shape
    d_out = w.shape[1]
    row = lambda width: pl.BlockSpec((tm, width), lambda i: (i, 0))
    widths = (d_sb, d_sb, d_mb, d_mb, 3 * d_sb, 3 * d_mb, d_mb, d_ssm)
    dtypes = (F32, F32, F32, F32, BF16, BF16, F32, F32)
    out_specs = [row(wd) for wd in widths]
    out_shape = [jax.ShapeDtypeStruct((n, wd), dt) for wd, dt in zip(widths, dtypes)]
    if with_ksum:
        nblk = tm // MOBA_BLOCK
        out_specs.append(pl.BlockSpec((1, nblk, d_mb), lambda i: (i, 0, 0)))
        out_shape.append(jax.ShapeDtypeStruct((n // tm, nblk, d_mb), F32))
    return pl.pallas_call(
        functools.partial(_in_proj_kernel, d_sb=d_sb, d_mb=d_mb, d_ssm=d_ssm),
        grid=(n // tm,),
        in_specs=[row(d), _resident((d, d_out)), _resident((1, d_out))],
        out_specs=out_specs,
        out_shape=out_shape,
        compiler_params=_cparams(("parallel",)),
        name="in_proj",
    )(x, w, b)


def _sb_attn_kernel(q_ref, k_ref, v_ref, o_ref, *, tile):
    qi = pl.program_id(2)
    q = q_ref[...]
    lane_head = lax.broadcasted_iota(jnp.int32, (tile, LANES), 1) // HEAD_DIM
    row = lax.broadcasted_iota(jnp.int32, (tile, tile), 0)
    col = lax.broadcasted_iota(jnp.int32, (tile, tile), 1)
    tri = (row >= col).astype(BF16)

    def head_out(h):
        qh = jnp.where(lane_head == h, q, jnp.zeros_like(q))

        def cond(st):
            kt, carry, _ = st
            return jnp.logical_and(kt >= 0, jnp.max(carry) > F32_EXP_ZERO)

        def body(st):
            kt, carry, acc = st
            start = pl.multiple_of(kt * tile, tile)
            k = k_ref[pl.ds(start, tile), :]
            v = v_ref[pl.ds(start, tile), :]
            z = _dot_nt(qh, k)
            valid = jnp.logical_or(kt < qi, col < row)
            lk = jnp.where(valid, _neg_softplus(z), 0.0)
            hi, lo = _split_bf16(lk)
            rcs = _dot(hi, tri) + _dot(lo, tri)
            w = jnp.where(valid, jnp.exp(z + carry + rcs), 0.0)
            acc = acc + _dot(w.astype(BF16), v)
            return kt - 1, carry + rcs[:, 0:1], acc

        init = (qi, jnp.zeros((tile, 1), F32), jnp.zeros((tile, LANES), F32))
        return lax.while_loop(cond, body, init)[2]

    out = jnp.where(lane_head == 0, head_out(0), head_out(1))
    o_ref[...] = out.astype(o_ref.dtype)


def _sb_attn(sb16, bsz, t, n_heads):
    tile = min(256, t)
    d_sb = n_heads * HEAD_DIM
    nlb = d_sb // LANES
    x = sb16.reshape(bsz, t, 3 * d_sb)
    return pl.pallas_call(
        functools.partial(_sb_attn_kernel, tile=tile),
        grid=(bsz, nlb, t // tile),
        in_specs=[pl.BlockSpec((None, tile, LANES), lambda b, p, i: (b, i, p)),
                  pl.BlockSpec((None, t, LANES), lambda b, p, i: (b, 0, nlb + p)),
                  pl.BlockSpec((None, t, LANES), lambda b, p, i: (b, 0, 2 * nlb + p))],
        out_specs=pl.BlockSpec((None, tile, LANES), lambda b, p, i: (b, i, p)),
        out_shape=jax.ShapeDtypeStruct((bsz, t, d_sb), BF16),
        compiler_params=_cparams(("parallel", "parallel", "arbitrary")),
        name="sb_attn",
    )(x, x, x)


def _moba_select(score, blk, n_blocks):
    sel = jnp.zeros(score.shape, jnp.bool_)
    s = score
    for _ in range(min(MOBA_TOPK, n_blocks)):
        m = jnp.max(s, axis=-1, keepdims=True)
        idx = jnp.min(jnp.where(s == m, blk, n_blocks), axis=-1, keepdims=True)
        pick = blk == idx
        sel = jnp.logical_or(sel, jnp.logical_and(pick, jnp.abs(m) < jnp.inf))
        s = jnp.where(pick, -jnp.inf, s)
    return sel


def _moba_attn_kernel(q_ref, q32_ref, k_ref, v_ref, ksumt_ref, o_ref, *, n_blocks):
    qi = pl.program_id(2)
    tile = MOBA_BLOCK
    q = q_ref[...]
    q32 = q32_ref[...]
    lane_head = lax.broadcasted_iota(jnp.int32, (tile, LANES), 1) // HEAD_DIM
    row = lax.broadcasted_iota(jnp.int32, (tile, tile), 0)
    col = lax.broadcasted_iota(jnp.int32, (tile, tile), 1)
    blk = lax.broadcasted_iota(jnp.int32, (tile, n_blocks), 1)
    ksumt = ksumt_ref[...]

    def head_out(h):
        hm = lane_head == h
        qh = jnp.where(hm, q, jnp.zeros_like(q))
        qh32 = jnp.where(hm, q32, 0.0)
        score = jnp.dot(qh32, ksumt, precision=_HI, preferred_element_type=F32) * (1.0 / MOBA_BLOCK)
        score = jnp.where(blk < qi, score, -jnp.inf)
        selbias = jnp.where(_moba_select(score, blk, n_blocks), 0.0, NEG_BIG)

        start = pl.multiple_of(qi * tile, tile)
        z = jnp.where(col <= row, _dot_nt(qh, k_ref[pl.ds(start, tile), :]), NEG_BIG)
        m = jnp.max(z, axis=-1, keepdims=True)
        p = jnp.exp(z - m)
        l = jnp.sum(p, axis=-1, keepdims=True)
        acc = _dot(p.astype(BF16), v_ref[pl.ds(start, tile), :])

        def body(kb, st):
            m, l, acc = st
            start = pl.multiple_of(kb * tile, tile)
            bias = jnp.sum(jnp.where(blk == kb, selbias, 0.0), axis=-1, keepdims=True)
            z = _dot_nt(qh, k_ref[pl.ds(start, tile), :]) + bias
            m_new = jnp.maximum(m, jnp.max(z, axis=-1, keepdims=True))
            corr = jnp.exp(m - m_new)
            p = jnp.exp(z - m_new)
            l = l * corr + jnp.sum(p, axis=-1, keepdims=True)
            acc = acc * corr + _dot(p.astype(BF16), v_ref[pl.ds(start, tile), :])
            return m_new, l, acc

        m, l, acc = lax.fori_loop(0, qi, body, (m, l, acc))
        return acc / l

    out = jnp.where(lane_head == 0, head_out(0), head_out(1))
    o_ref[...] = out.astype(o_ref.dtype)


def _moba_attn(mb16, q_mb32, ksum_t, bsz, t, n_heads):
    tile = MOBA_BLOCK
    d_mb = n_heads * HEAD_DIM
    nlb = d_mb // LANES
    n_blocks = t // tile
    x = mb16.reshape(bsz, t, 3 * d_mb)
    q32 = q_mb32.reshape(bsz, t, d_mb)
    return pl.pallas_call(
        functools.partial(_moba_attn_kernel, n_blocks=n_blocks),
        grid=(bsz, nlb, n_blocks),
        in_specs=[pl.BlockSpec((None, tile, LANES), lambda b, p, i: (b, i, p)),
                  pl.BlockSpec((None, tile, LANES), lambda b, p, i: (b, i, p)),
                  pl.BlockSpec((None, t, LANES), lambda b, p, i: (b, 0, nlb + p)),
                  pl.BlockSpec((None, t, LANES), lambda b, p, i: (b, 0, 2 * nlb + p)),
                  pl.BlockSpec((None, LANES, n_blocks), lambda b, p, i: (b, p, 0))],
        out_specs=pl.BlockSpec((None, tile, LANES), lambda b, p, i: (b, i, p)),
        out_shape=jax.ShapeDtypeStruct((bsz, t, d_mb), BF16),
        compiler_params=_cparams(("parallel", "parallel", "arbitrary")),
        name="moba_attn",
    )(x, q32, x, x, ksum_t)


def _ssm_tables(a_re, a_im, log_dt, b_re, b_im, c_re, c_im, ssm_d):
    ng, ns = a_re.shape
    nc = SSM_GROUP_CH
    L = SSM_CHUNK
    a_re, a_im = a_re.astype(F32), a_im.astype(F32)
    dt = jnp.exp(log_dt.astype(F32))[:, None]
    tau = jnp.arange(L + 1, dtype=F32)[:, None, None]
    mag = jnp.exp(tau * (a_re * dt))
    ang = tau * (a_im * dt)
    pw_re, pw_im = mag * jnp.cos(ang), mag * jnp.sin(ang)
    lam_re, lam_im = pw_re[1], pw_im[1]
    den = a_re * a_re + a_im * a_im
    num_re, num_im = lam_re - 1.0, lam_im
    coef_re = (num_re * a_re + num_im * a_im) / den
    coef_im = (num_im * a_re - num_re * a_im) / den
    b_re, b_im = b_re.astype(F32), b_im.astype(F32)
    bb_re = coef_re[..., None] * b_re - coef_im[..., None] * b_im
    bb_im = coef_re[..., None] * b_im + coef_im[..., None] * b_re
    c_re, c_im = c_re.astype(F32), c_im.astype(F32)
    x_re = pw_re[..., None] * bb_re - pw_im[..., None] * bb_im
    x_im = pw_re[..., None] * bb_im + pw_im[..., None] * bb_re
    kern = (jnp.einsum('gcp,tgpd->tgcd', c_re, x_re, precision=_HI)
            - jnp.einsum('gcp,tgpd->tgcd', c_im, x_im, precision=_HI))
    s_idx = jnp.arange(L)[:, None]
    t_idx = jnp.arange(L)[None, :]
    delta = t_idx - s_idx
    toep = jnp.where((delta >= 0)[:, :, None, None, None], kern[jnp.clip(delta, 0, L)], 0.0)
    m_intra = jnp.transpose(toep, (2, 0, 4, 1, 3)).reshape(ng, L * nc, L * nc)
    ws_re = jnp.transpose(x_re[L - 1 - jnp.arange(L)], (1, 0, 3, 2)).reshape(ng, L * nc, ns)
    ws_im = jnp.transpose(x_im[L - 1 - jnp.arange(L)], (1, 0, 3, 2)).reshape(ng, L * nc, ns)
    w_state = jnp.concatenate([ws_re, ws_im, ws_im, ws_re], axis=-1)
    cl_re = c_re[None] * jnp.transpose(pw_re[1:], (0, 1, 2))[:, :, None, :] \
        - c_im[None] * pw_im[1:][:, :, None, :]
    cl_im = c_re[None] * pw_im[1:][:, :, None, :] + c_im[None] * pw_re[1:][:, :, None, :]
    wh_re = jnp.transpose(cl_re, (1, 3, 0, 2)).reshape(ng, ns, L * nc)
    wh_im = jnp.transpose(cl_im, (1, 3, 0, 2)).reshape(ng, ns, L * nc)
    w_carry = jnp.concatenate([wh_re, -wh_im], axis=1)
    lr, li = pw_re[L], pw_im[L]
    dec_a = jnp.concatenate([lr, lr], axis=-1).reshape(1, 2 * ng * ns)
    dec_p = jnp.concatenate([-li, li], axis=-1).reshape(1, 2 * ng * ns)
    d_tile = jnp.tile(ssm_d.astype(F32).reshape(ng, 1, nc), (1, 1, L))
    return dict(m_intra=m_intra.astype(BF16), w_state=w_state.astype(BF16),
                w_carry=w_carry.astype(BF16), dec_a=dec_a, dec_p=dec_p, d_tile=d_tile,
                lam_re=lam_re, lam_im=lam_im, bb_re=bb_re, bb_im=bb_im)


def _ssm_prompt_kernel(u_ref, mi_ref, ws_ref, wc_ref, da_ref, dp_ref, dt_ref,
                       y_ref, hfin_ref, sp_ref, sq_ref, hprev_ref, p_ref, q_ref,
                       *, n_groups, n_state, bp, n_chunks):
    step = pl.program_id(0)
    sw = 2 * n_state

    @pl.when(step == 0)
    def _():
        p_ref[...] = jnp.zeros_like(p_ref)
        q_ref[...] = jnp.zeros_like(q_ref)

    for g in range(n_groups):
        ug = u_ref[g]
        ugb = ug.astype(BF16)
        y_ref[g] = _dot(ugb, mi_ref[g]) + ug * dt_ref[g]
        s4 = _dot(ugb, ws_ref[g])
        sp_ref[:, g * sw:(g + 1) * sw] = s4[:, :sw]
        sq_ref[:, g * sw:(g + 1) * sw] = s4[:, sw:]

    dec_a = da_ref[...]
    dec_p = dp_ref[...]

    def scan(c, st):
        p, q = st
        r0 = pl.multiple_of(c * bp, bp)
        hprev_ref[pl.ds(r0, bp), :] = p
        p_new = dec_a * p + dec_p * q + sp_ref[pl.ds(r0, bp), :]
        q_new = dec_a * q - dec_p * p + sq_ref[pl.ds(r0, bp), :]
        return p_new, q_new

    p, q = lax.fori_loop(0, n_chunks, scan, (p_ref[...], q_ref[...]))
    p_ref[...] = p
    q_ref[...] = q
    hfin_ref[...] = p

    for g in range(n_groups):
        hg = hprev_ref[:, g * sw:(g + 1) * sw].astype(BF16)
        y_ref[g] = y_ref[g] + _dot(hg, wc_ref[g])


def _ssm_prompt(u, tabs, bsz, t, n_groups, n_state):
    L, nc = SSM_CHUNK, SSM_GROUP_CH
    bp = -(-bsz // SUBLANES) * SUBLANES
    n_chunk_total = t // L
    n_chunks = min(32, n_chunk_total)
    rows = n_chunks * bp
    width = L * nc
    ul = jnp.transpose(u.reshape(bsz, n_chunk_total, L, n_groups, nc), (3, 1, 0, 2, 4))
    ul = jnp.pad(ul, ((0, 0), (0, 0), (0, bp - bsz), (0, 0), (0, 0)))
    ul = ul.reshape(n_groups, n_chunk_total * bp, width)
    sl = 2 * n_state * n_groups
    y, hfin = pl.pallas_call(
        functools.partial(_ssm_prompt_kernel, n_groups=n_groups, n_state=n_state, bp=bp,
                          n_chunks=n_chunks),
        grid=(n_chunk_total // n_chunks,),
        in_specs=[pl.BlockSpec((n_groups, rows, width), lambda i: (0, i, 0)),
                  _resident((n_groups, width, width)),
                  _resident((n_groups, width, 4 * n_state)),
                  _resident((n_groups, 2 * n_state, width)),
                  _resident((1, sl)), _resident((1, sl)),
                  _resident((n_groups, 1, width))],
        out_specs=[pl.BlockSpec((n_groups, rows, width), lambda i: (0, i, 0)),
                   pl.BlockSpec((bp, sl), lambda i: (0, 0))],
        out_shape=[jax.ShapeDtypeStruct((n_groups, n_chunk_total * bp, width), F32),
                   jax.ShapeDtypeStruct((bp, sl), F32)],
        scratch_shapes=[pltpu.VMEM((rows, sl), F32), pltpu.VMEM((rows, sl), F32),
                        pltpu.VMEM((rows, sl), F32),
                        pltpu.VMEM((bp, sl), F32), pltpu.VMEM((bp, sl), F32)],
        compiler_params=_cparams(("arbitrary",)),
        name="ssm_prompt",
    )(ul, tabs['m_intra'], tabs['w_state'], tabs['w_carry'], tabs['dec_a'], tabs['dec_p'],
      tabs['d_tile'])
    y = y.reshape(n_groups, n_chunk_total, bp, L, nc)[:, :, :bsz]
    y = jnp.transpose(y, (2, 1, 3, 0, 4)).reshape(bsz * t, n_groups * nc)
    h = hfin[:bsz].reshape(bsz, n_groups, 2, n_state)
    return y, h[:, :, 0], h[:, :, 1]


def _ssm_step_kernel(u_ref, hr_ref, hi_ref, wbr_ref, wbi_ref, lr_ref, li_ref,
                     wcr_ref, wci_ref, d_ref, y_ref, hro_ref, hio_ref):
    u = u_ref[...]
    dot = lambda a, b: jnp.dot(a, b, precision=_HI, preferred_element_type=F32)
    h0r, h0i = hr_ref[...], hi_ref[...]
    lr, li = lr_ref[...], li_ref[...]
    hr = dot(u, wbr_ref[...]) + (lr * h0r - li * h0i)
    hi = dot(u, wbi_ref[...]) + (lr * h0i + li * h0r)
    hro_ref[...] = hr
    hio_ref[...] = hi
    y_ref[...] = dot(hr, wcr_ref[...]) - dot(hi, wci_ref[...]) + d_ref[...] * u


def _ssm_step(u, h0_re, h0_im, tabs, c_re, c_im, ssm_d):
    bsz, ng, ns = h0_re.shape
    nc = SSM_GROUP_CH
    eye = jnp.eye(ng, dtype=F32)
    blockdiag = lambda w: jnp.einsum('gab,gh->gahb', w, eye).reshape(ng * w.shape[1], ng * w.shape[2])
    wb_re = blockdiag(jnp.swapaxes(tabs['bb_re'], 1, 2))
    wb_im = blockdiag(jnp.swapaxes(tabs['bb_im'], 1, 2))
    wc_re = blockdiag(jnp.swapaxes(c_re.astype(F32), 1, 2))
    wc_im = blockdiag(jnp.swapaxes(c_im.astype(F32), 1, 2))
    flat = lambda a: a.astype(F32).reshape(1, ng * ns)
    y, hr, hi = pl.pallas_call(
        _ssm_step_kernel,
        out_shape=[jax.ShapeDtypeStruct((bsz, ng * nc), F32),
                   jax.ShapeDtypeStruct((bsz, ng * ns), F32),
                   jax.ShapeDtypeStruct((bsz, ng * ns), F32)],
        compiler_params=pltpu.CompilerParams(vmem_limit_bytes=VMEM_LIMIT),
        name="ssm_step",
    )(u, h0_re.astype(F32).reshape(bsz, ng * ns), h0_im.astype(F32).reshape(bsz, ng * ns),
      wb_re, wb_im, flat(tabs['lam_re']), flat(tabs['lam_im']), wc_re, wc_im,
      ssm_d.astype(F32).reshape(1, ng * nc))
    return y, hr.reshape(bsz, ng, ns), hi.reshape(bsz, ng, ns)


def _merge_kernel(x_ref, osb_ref, omb_ref, yssm_ref, wg_ref, bg_ref, wglu_ref,
                  wbsb_ref, wbmb_ref, wbssm_ref, wout_ref, g_ref, b_ref, o_ref, *, alpha):
    x = x_ref[...]
    d = x.shape[1]
    gates = jax.nn.sigmoid(_dot(x.astype(BF16), wg_ref[...]) + bg_ref[...])
    glu = _dot(yssm_ref[...].astype(BF16), wglu_ref[...])
    half = glu.shape[1] // 2
    o_ssm = glu[:, :half] * jax.nn.sigmoid(glu[:, half:])
    merged = (gates[:, 0:d] * _dot(osb_ref[...], wbsb_ref[...])
              + gates[:, d:2 * d] * _dot(omb_ref[...], wbmb_ref[...])
              + gates[:, 2 * d:3 * d] * _dot(o_ssm.astype(BF16), wbssm_ref[...]))
    mix = _dot(merged.astype(BF16), wout_ref[...])
    o_ref[...] = _layer_norm(alpha * x + mix, g_ref[...], b_ref[...])


def _merge(x, o_sb, o_mb, y_ssm, lw, alpha, tm):
    n, d = x.shape
    row = lambda a: pl.BlockSpec((tm, a.shape[1]), lambda i: (i, 0))
    acts = (x, o_sb, o_mb, y_ssm)
    weights = (lw['w_gate'], lw['b_gate'], lw['w_glu'], lw['w_br_sb'], lw['w_br_mb'],
               lw['w_br_ssm'], lw['w_out'], lw['ln2_g'], lw['ln2_b'])
    return pl.pallas_call(
        functools.partial(_merge_kernel, alpha=alpha),
        grid=(n // tm,),
        in_specs=[row(a) for a in acts] + [_resident(w.shape) for w in weights],
        out_specs=pl.BlockSpec((tm, d), lambda i: (i, 0)),
        out_shape=jax.ShapeDtypeStruct((n, d), F32),
        compiler_params=_cparams(("parallel",)),
        name="merge",
    )(*acts, *weights)


def _sb_decode_kernel(pt_ref, q_ref, *refs, n_heads, pages_per_step):
    k_refs = refs[:pages_per_step]
    v_refs = refs[pages_per_step:2 * pages_per_step]
    o_ref = refs[2 * pages_per_step]
    carry_ref, acc_ref = refs[2 * pages_per_step + 1:]
    j = pl.program_id(1)
    page = k_refs[0].shape[0]
    d = n_heads * HEAD_DIM

    @pl.when(j == 0)
    def _():
        carry_ref[...] = jnp.zeros_like(carry_ref)
        acc_ref[...] = jnp.zeros_like(acc_ref)

    q = q_ref[...] * (HEAD_DIM ** -0.5)
    sub_head = lax.broadcasted_iota(jnp.int32, (LANES, d), 0)
    lane_head = lax.broadcasted_iota(jnp.int32, (LANES, d), 1) // HEAD_DIM
    q_rows = jnp.where(sub_head == lane_head, jnp.broadcast_to(q, (LANES, d)), 0.0).astype(BF16)
    row = lax.broadcasted_iota(jnp.int32, (page, page), 0)
    col = lax.broadcasted_iota(jnp.int32, (page, page), 1)
    tri = (col >= row).astype(BF16)
    lane = lax.broadcasted_iota(jnp.int32, (1, LANES), 1)

    for i in range(pages_per_step):
        carry = carry_ref[...]
        alive = jnp.max(jnp.where(lane < n_heads, carry, -jnp.inf)) > F32_EXP_ZERO

        @pl.when(alive)
        def _():
            k = k_refs[i][...].astype(BF16)
            v = v_refs[i][...].astype(BF16)
            zt = _dot_nt(k, q_rows)
            lk = _neg_softplus(zt)
            hi, lo = _split_bf16(lk)
            rcs = _dot(tri, hi) + _dot(tri, lo)
            w = jnp.exp(zt + carry + rcs)
            wv = _dot(w.T.astype(BF16), v)
            acc_ref[...] += wv[:SUBLANES, :]
            carry_ref[...] = carry + rcs[0:1, :]

    @pl.when(j == pl.num_programs(1) - 1)
    def _():
        acc = acc_ref[...]
        sh = lax.broadcasted_iota(jnp.int32, (SUBLANES, d), 0)
        lh = lax.broadcasted_iota(jnp.int32, (SUBLANES, d), 1) // HEAD_DIM
        o_ref[...] = jnp.sum(jnp.where(sh == lh, acc, 0.0), axis=0, keepdims=True).astype(o_ref.dtype)


def _sb_decode(q, cache_k, cache_v, layer, page_table, n_heads):
    bsz, n_pages = page_table.shape
    depth, n_pool, page = cache_k.shape[:3]
    d = n_heads * HEAD_DIM
    assert n_heads <= SUBLANES
    pps = math.gcd(n_pages, 8)
    ck = cache_k.reshape(depth, n_pool, page, d)
    cv = cache_v.reshape(depth, n_pool, page, d)

    def page_spec(i):
        return pl.BlockSpec((None, None, page, d),
                            lambda b, j, pt: (layer, pt[b, n_pages - 1 - (j * pps + i)], 0, 0))

    grid_spec = pltpu.PrefetchScalarGridSpec(
        num_scalar_prefetch=1,
        grid=(bsz, n_pages // pps),
        in_specs=[pl.BlockSpec((None, 1, d), lambda b, j, pt: (b, 0, 0))]
        + [page_spec(i) for i in range(pps)] * 2,
        out_specs=pl.BlockSpec((None, 1, d), lambda b, j, pt: (b, 0, 0)),
        scratch_shapes=[pltpu.VMEM((1, LANES), F32), pltpu.VMEM((SUBLANES, d), F32)],
    )
    out = pl.pallas_call(
        functools.partial(_sb_decode_kernel, n_heads=n_heads, pages_per_step=pps),
        grid_spec=grid_spec,
        out_shape=jax.ShapeDtypeStruct((bsz, 1, d), BF16),
        compiler_params=_cparams(("parallel", "arbitrary")),
        name="sb_decode",
    )(page_table, q.reshape(bsz, 1, d), *([ck] * pps), *([cv] * pps))
    return out.reshape(bsz, d)


def _moba_ksum_kernel(pt_ref, *refs, pages_per_step, pages_per_block):
    k_refs = refs[:pages_per_step]
    o_ref = refs[pages_per_step]
    for m in range(pages_per_step // pages_per_block):
        s = jnp.sum(k_refs[m * pages_per_block][...], axis=0, keepdims=True)
        for r in range(1, pages_per_block):
            s = s + jnp.sum(k_refs[m * pages_per_block + r][...], axis=0, keepdims=True)
        o_ref[m:m + 1, :] = s


def _moba_ksum(cache_k, layer, page_table, n_heads):
    bsz, n_pages = page_table.shape
    depth, n_pool, page = cache_k.shape[:3]
    d = n_heads * HEAD_DIM
    ppb = MOBA_BLOCK // page
    n_blocks = n_pages // ppb
    bps = math.gcd(n_blocks, SUBLANES)
    assert bps == SUBLANES or bps == n_blocks
    pps = bps * ppb
    ck = cache_k.reshape(depth, n_pool, page, d)

    def page_spec(i):
        return pl.BlockSpec((None, None, page, d),
                            lambda b, j, pt: (layer, pt[b, j * pps + i], 0, 0))

    grid_spec = pltpu.PrefetchScalarGridSpec(
        num_scalar_prefetch=1,
        grid=(bsz, n_pages // pps),
        in_specs=[page_spec(i) for i in range(pps)],
        out_specs=pl.BlockSpec((None, bps, d), lambda b, j, pt: (b, j, 0)),
    )
    return pl.pallas_call(
        functools.partial(_moba_ksum_kernel, pages_per_step=pps, pages_per_block=ppb),
        grid_spec=grid_spec,
        out_shape=jax.ShapeDtypeStruct((bsz, n_blocks, d), F32),
        compiler_params=_cparams(("parallel", "parallel")),
        name="moba_ksum",
    )(page_table, *([ck] * pps))


def _moba_topk_kernel(q_ref, ksum_ref, idx_ref, ok_ref, *, n_heads, n_blocks):
    d = n_heads * HEAD_DIM
    prod = ksum_ref[...] * q_ref[...]
    seg = (lax.broadcasted_iota(jnp.int32, (d, LANES), 0) // HEAD_DIM
           == lax.broadcasted_iota(jnp.int32, (d, LANES), 1)).astype(F32)
    score = jnp.dot(prod, seg, precision=_HI, preferred_element_type=F32) * (1.0 / MOBA_BLOCK)
    blk = lax.broadcasted_iota(jnp.int32, (n_blocks, LANES), 0)
    s = score
    idx_rows, ok_rows = [], []
    for _ in range(MOBA_TOPK):
        m = jnp.max(s, axis=0, keepdims=True)
        idx = jnp.min(jnp.where(s == m, blk, n_blocks), axis=0, keepdims=True)
        idx_rows.append(jnp.minimum(idx, n_blocks - 1))
        ok_rows.append((jnp.abs(m) < jnp.inf).astype(jnp.int32))
        s = jnp.where(blk == idx, -jnp.inf, s)
    pad = [jnp.zeros((1, LANES), jnp.int32)] * (SUBLANES - MOBA_TOPK)
    idx_ref[...] = jnp.concatenate(idx_rows + pad, axis=0)
    ok_ref[...] = jnp.concatenate(ok_rows + pad, axis=0)


def _moba_topk(q, ksum, n_heads):
    bsz, n_blocks, d = ksum.shape
    assert n_blocks >= MOBA_TOPK
    idx, ok = pl.pallas_call(
        functools.partial(_moba_topk_kernel, n_heads=n_heads, n_blocks=n_blocks),
        grid=(bsz,),
        in_specs=[pl.BlockSpec((None, 1, d), lambda b: (b, 0, 0)),
                  pl.BlockSpec((None, n_blocks, d), lambda b: (b, 0, 0))],
        out_specs=[pl.BlockSpec((None, SUBLANES, LANES), lambda b: (b, 0, 0))] * 2,
        out_shape=[jax.ShapeDtypeStruct((bsz, SUBLANES, LANES), jnp.int32)] * 2,
        compiler_params=_cparams(("parallel",)),
        name="moba_topk",
    )(q.reshape(bsz, 1, d), ksum)
    return (jnp.transpose(idx[:, :MOBA_TOPK, :n_heads], (0, 2, 1)),
            jnp.transpose(ok[:, :MOBA_TOPK, :n_heads], (0, 2, 1)))


def _moba_decode_kernel(pt_ref, idx_ref, ok_ref, q_ref, kn_ref, vn_ref, *refs, n_tiles):
    per_pair = HEADS_PER_VREG * n_tiles
    k_refs = refs[:per_pair]
    v_refs = refs[per_pair:2 * per_pair]
    o_ref = refs[2 * per_pair]
    b = pl.program_id(0)
    p = pl.program_id(1)
    n_heads = pl.num_programs(1) * HEADS_PER_VREG
    ppb = n_tiles // MOBA_TOPK
    lane_head = lax.broadcasted_iota(jnp.int32, (1, LANES), 1) // HEAD_DIM
    q = q_ref[...] * (HEAD_DIM ** -0.5)
    kn = kn_ref[...]
    vn = vn_ref[...]
    out = jnp.zeros((1, LANES), F32)
    for hh in range(HEADS_PER_VREG):
        hm = lane_head == hh
        qh = jnp.where(hm, q, 0.0)
        z_own = jnp.sum(qh * kn, axis=-1, keepdims=True)
        zs = []
        for t in range(n_tiles):
            ok = ok_ref[(b * n_heads + p * HEADS_PER_VREG + hh) * MOBA_TOPK + t // ppb] > 0
            z = jnp.sum(k_refs[hh * n_tiles + t][...] * qh, axis=-1, keepdims=True)
            zs.append(jnp.where(ok, z, NEG_BIG))
        m = z_own
        for z in zs:
            m = jnp.maximum(m, jnp.max(z, axis=0, keepdims=True))
        l = jnp.ones((1, 1), F32) * jnp.exp(z_own - m)
        acc = jnp.exp(z_own - m) * vn
        for t in range(n_tiles):
            pw = jnp.exp(zs[t] - m)
            l = l + jnp.sum(pw, axis=0, keepdims=True)
            acc = acc + jnp.sum(pw * v_refs[hh * n_tiles + t][...], axis=0, keepdims=True)
        out = jnp.where(hm, acc / l, out)
    o_ref[...] = out.astype(o_ref.dtype)


def _moba_decode(q, k_new, v_new, idx, ok, cache_k, cache_v, layer, page_table, n_heads):
    bsz, n_pages = page_table.shape
    depth, n_pool, page = cache_k.shape[:3]
    d = n_heads * HEAD_DIM
    ppb = MOBA_BLOCK // page
    n_tiles = MOBA_TOPK * ppb
    ck = cache_k.reshape(depth, n_pool, page, d)
    cv = cache_v.reshape(depth, n_pool, page, d)

    def tile_spec(hh, t):
        def index_map(b, p, pt, idx, ok):
            blk = idx[(b * n_heads + p * HEADS_PER_VREG + hh) * MOBA_TOPK + t // ppb]
            return (layer, pt[b, blk * ppb + t % ppb], 0, p)
        return pl.BlockSpec((None, None, page, LANES), index_map)

    tiles = [tile_spec(hh, t) for hh in range(HEADS_PER_VREG) for t in range(n_tiles)]
    vec = pl.BlockSpec((None, 1, LANES), lambda b, p, pt, idx, ok: (b, 0, p))
    grid_spec = pltpu.PrefetchScalarGridSpec(
        num_scalar_prefetch=3,
        grid=(bsz, d // LANES),
        in_specs=[vec, vec, vec] + tiles * 2,
        out_specs=vec,
    )
    n_t = len(tiles)
    out = pl.pallas_call(
        functools.partial(_moba_decode_kernel, n_tiles=n_tiles),
        grid_spec=grid_spec,
        out_shape=jax.ShapeDtypeStruct((bsz, 1, d), BF16),
        compiler_params=_cparams(("parallel", "parallel")),
        name="moba_decode",
    )(page_table, idx.reshape(-1), ok.reshape(-1), q.reshape(bsz, 1, d), k_new.reshape(bsz, 1, d),
      v_new.reshape(bsz, 1, d), *([ck] * n_t), *([cv] * n_t))
    return out.reshape(bsz, d)


def _row_tile(n):
    for tm in (512, 256, 128, 64, 32, 16, 8):
        if n % tm == 0:
            return tm
    return n


def _layer_weights(w, l, d_model, d_sb, d_mb, d_ssm):
    n_qkvu = 3 * d_sb + 3 * d_mb + d_ssm
    bf = lambda a: a.astype(BF16)
    vec = lambda a: a.astype(F32).reshape(1, -1)
    lw = dict(
        w_ffn1_up=bf(w['w_ffn1_up'][l]), w_ffn1_down=bf(w['w_ffn1_down'][l]),
        w_ffn2_up=bf(w['w_ffn2_up'][l]), w_ffn2_down=bf(w['w_ffn2_down'][l]),
        w_qkvu=bf(w['w_in'][l][:, :n_qkvu]), b_qkvu=vec(w['b_in'][l][:n_qkvu]),
        w_gate=bf(w['w_in'][l][:, n_qkvu:]), b_gate=vec(w['b_in'][l][n_qkvu:]),
        w_glu=bf(w['w_glu'][l]), w_br_sb=bf(w['w_br_sb'][l]), w_br_mb=bf(w['w_br_mb'][l]),
        w_br_ssm=bf(w['w_br_ssm'][l]), w_out=bf(w['w_out'][l]),
    )
    for name in ('ln1_g', 'ln1_b', 'ln2_g', 'ln2_b', 'ln3_g', 'ln3_b'):
        lw[name] = vec(w[name][l])
    return lw


def kernel(x_prompt, x_sample, cache_k_sb, cache_v_sb, cache_k_mb, cache_v_mb, state_ssm_re, state_ssm_im, page_table, ln1_g, ln1_b, w_ffn1_up, w_ffn1_down, w_in, b_in, ssm_a_re, ssm_a_im, ssm_log_dt, ssm_b_re, ssm_b_im, ssm_c_re, ssm_c_im, ssm_d, w_glu, w_br_sb, w_br_mb, w_br_ssm, w_out, ln2_g, ln2_b, w_ffn2_up, w_ffn2_down, ln3_g, ln3_b):
    weights = dict(ln1_g=ln1_g, ln1_b=ln1_b, w_ffn1_up=w_ffn1_up, w_ffn1_down=w_ffn1_down,
                   w_in=w_in, b_in=b_in, w_glu=w_glu, w_br_sb=w_br_sb, w_br_mb=w_br_mb,
                   w_br_ssm=w_br_ssm, w_out=w_out, ln2_g=ln2_g, ln2_b=ln2_b,
                   w_ffn2_up=w_ffn2_up, w_ffn2_down=w_ffn2_down, ln3_g=ln3_g, ln3_b=ln3_b)
    depth = w_in.shape[0]
    bsz, t, d_model = x_prompt.shape
    dbsz, dec_t, _ = x_sample.shape
    h_sb, h_mb = cache_k_sb.shape[3], cache_k_mb.shape[3]
    n_groups, n_state = ssm_a_re.shape[1:]
    d_sb, d_mb, d_ssm = h_sb * HEAD_DIM, h_mb * HEAD_DIM, n_groups * SSM_GROUP_CH
    page = cache_k_sb.shape[2]
    past_len = page_table.shape[1] * page
    assert dec_t == 1 and t % MOBA_BLOCK == 0 and t % SSM_CHUNK == 0
    assert MOBA_BLOCK % page == 0 and past_len % MOBA_BLOCK == 0
    assert d_sb % LANES == 0 and d_mb % LANES == 0
    alpha = float((2 * depth) ** 0.25)

    n_p = bsz * t
    tm_p = _row_tile(n_p)
    if tm_p % MOBA_BLOCK:
        tm_p = MOBA_BLOCK
    tm_s = _row_tile(dbsz)

    y_p = x_prompt.reshape(n_p, d_model)
    y_s = x_sample.reshape(dbsz, d_model)
    prompt_states, sample_states = [], []
    for l in range(depth):
        lw = _layer_weights(weights, l, d_model, d_sb, d_mb, d_ssm)
        tabs = _ssm_tables(ssm_a_re[l], ssm_a_im[l], ssm_log_dt[l], ssm_b_re[l], ssm_b_im[l],
                           ssm_c_re[l], ssm_c_im[l], ssm_d[l])

        x1 = _ffn_ln(y_p, lw['w_ffn1_up'], lw['w_ffn1_down'], lw['ln1_g'], lw['ln1_b'], alpha, tm_p)
        (k_sb, v_sb, k_mb, v_mb, sb16, mb16, q_mb32, u, ksum) = _in_proj(
            x1, lw['w_qkvu'], lw['b_qkvu'], d_sb, d_mb, d_ssm, tm_p, True)
        o_sb = _sb_attn(sb16, bsz, t, h_sb).reshape(n_p, d_sb)
        ksum_t = jnp.swapaxes(ksum.reshape(bsz, t // MOBA_BLOCK, d_mb), 1, 2)
        o_mb = _moba_attn(mb16, q_mb32, ksum_t, bsz, t, h_mb).reshape(n_p, d_mb)
        y_ssm, h_re, h_im = _ssm_prompt(u, tabs, bsz, t, n_groups, n_state)
        x2 = _merge(x1, o_sb, o_mb, y_ssm, lw, alpha, tm_p)
        y_p = _ffn_ln(x2, lw['w_ffn2_up'], lw['w_ffn2_down'], lw['ln3_g'], lw['ln3_b'], alpha, tm_p)
        prompt_states.append((k_sb.reshape(bsz, t, h_sb, HEAD_DIM), v_sb.reshape(bsz, t, h_sb, HEAD_DIM),
                              k_mb.reshape(bsz, t, h_mb, HEAD_DIM), v_mb.reshape(bsz, t, h_mb, HEAD_DIM),
                              h_re.astype(state_ssm_re.dtype), h_im.astype(state_ssm_im.dtype)))

        s1 = _ffn_ln(y_s, lw['w_ffn1_up'], lw['w_ffn1_down'], lw['ln1_g'], lw['ln1_b'], alpha, tm_s)
        (k_sb, v_sb, k_mb, v_mb, sb16, mb16, q_mb32, u) = _in_proj(
            s1, lw['w_qkvu'], lw['b_qkvu'], d_sb, d_mb, d_ssm, tm_s, False)
        q_sb32 = sb16[:, :d_sb].astype(F32) * (HEAD_DIM ** 0.5)
        o_sb = _sb_decode(q_sb32, cache_k_sb, cache_v_sb, l, page_table, h_sb)
        ksum = _moba_ksum(cache_k_mb, l, page_table, h_mb)
        idx, ok = _moba_topk(q_mb32, ksum, h_mb)
        o_mb = _moba_decode(q_mb32, k_mb, v_mb, idx, ok, cache_k_mb, cache_v_mb, l, page_table, h_mb)
        y_ssm, h_re, h_im = _ssm_step(u, state_ssm_re[l], state_ssm_im[l], tabs,
                                      ssm_c_re[l], ssm_c_im[l], ssm_d[l])
        s2 = _merge(s1, o_sb, o_mb, y_ssm, lw, alpha, tm_s)
        y_s = _ffn_ln(s2, lw['w_ffn2_up'], lw['w_ffn2_down'], lw['ln3_g'], lw['ln3_b'], alpha, tm_s)
        sample_states.append((k_sb.reshape(dbsz, 1, h_sb, HEAD_DIM), v_sb.reshape(dbsz, 1, h_sb, HEAD_DIM),
                              k_mb.reshape(dbsz, 1, h_mb, HEAD_DIM), v_mb.reshape(dbsz, 1, h_mb, HEAD_DIM),
                              h_re.astype(state_ssm_re.dtype), h_im.astype(state_ssm_im.dtype)))

    stack = lambda states: tuple(jnp.stack(group) for group in zip(*states))
    k_sb_p, v_sb_p, k_mb_p, v_mb_p, h_re_p, h_im_p = stack(prompt_states)
    k_sb_s, v_sb_s, k_mb_s, v_mb_s, h_re_s, h_im_s = stack(sample_states)
    return (y_p.reshape(bsz, t, d_model), y_s.reshape(dbsz, dec_t, d_model),
            k_sb_p, v_sb_p, k_mb_p, v_mb_p, h_re_p, h_im_p,
            k_sb_s, v_sb_s, k_mb_s, v_mb_s, h_re_s, h_im_s)
```

```python
import functools
import math

import jax
import jax.numpy as jnp
from jax import lax
from jax.experimental import pallas as pl
from jax.experimental.pallas import tpu as pltpu

F32 = jnp.float32
BF16 = jnp.bfloat16

HEAD_DIM = 64
SSM_GROUP_CH = 16
MOBA_BLOCK = 256
MOBA_TOPK = 3
LN_EPS = 1e-5
LANES = 128
SUBLANES = 8
HEADS_PER_VREG = LANES // HEAD_DIM
SSM_CHUNK = 16
NEG_BIG = -1e30
F32_EXP_ZERO = -104.0
VMEM_LIMIT = 56 * 1024 * 1024

_HI = lax.Precision.HIGHEST


def _cparams(sem):
    return pltpu.CompilerParams(dimension_semantics=sem, vmem_limit_bytes=VMEM_LIMIT)


def _resident(shape):
    return pl.BlockSpec(shape, lambda *_: (0,) * len(shape), pipeline_mode=pl.Buffered(1))


def _dot(a, b):
    return jnp.dot(a, b, preferred_element_type=F32)


def _dot_nt(a, b):
    return lax.dot_general(a, b, (((1,), (1,)), ((), ())), preferred_element_type=F32)


def _layer_norm(y, g, b):
    mu = jnp.mean(y, axis=-1, keepdims=True)
    d = y - mu
    var = jnp.mean(d * d, axis=-1, keepdims=True)
    return d * lax.rsqrt(var + LN_EPS) * g + b


def _neg_softplus(z):
    return -(jnp.maximum(z, 0.0) + jnp.log(1.0 + jnp.exp(-jnp.abs(z))))


def _split_bf16(x):
    hi = x.astype(BF16)
    lo = (x - hi.astype(F32)).astype(BF16)
    return hi, lo


def _ffn_ln_kernel(x_ref, wup_ref, wdn_ref, g_ref, b_ref, o_ref, *, alpha, d_ff, n_chunk):
    x = x_ref[...]
    xb = x.astype(BF16)
    fc = d_ff // n_chunk
    acc = jnp.zeros_like(x)
    for c in range(n_chunk):
        a = c * fc
        gate = _dot(xb, wup_ref[:, a:a + fc])
        up = _dot(xb, wup_ref[:, d_ff + a:d_ff + a + fc])
        h = (gate * jax.nn.sigmoid(gate) * up).astype(BF16)
        acc = acc + _dot(h, wdn_ref[a:a + fc, :])
    o_ref[...] = _layer_norm(alpha * x + 0.5 * acc, g_ref[...], b_ref[...])


def _ffn_ln(x, w_up, w_down, g, b, alpha, tm):
    n, d = x.shape
    d_ff = w_down.shape[0]
    n_chunk = 2 if d_ff % (2 * LANES) == 0 else 1
    return pl.pallas_call(
        functools.partial(_ffn_ln_kernel, alpha=alpha, d_ff=d_ff, n_chunk=n_chunk),
        grid=(n // tm,),
        in_specs=[pl.BlockSpec((tm, d), lambda i: (i, 0)),
                  _resident((d, 2 * d_ff)), _resident((d_ff, d)),
                  _resident((1, d)), _resident((1, d))],
        out_specs=pl.BlockSpec((tm, d), lambda i: (i, 0)),
        out_shape=jax.ShapeDtypeStruct((n, d), F32),
        compiler_params=_cparams(("parallel",)),
        name="ffn_ln",
    )(x, w_up, w_down, g, b)


def _in_proj_kernel(x_ref, w_ref, b_ref, ksb_ref, vsb_ref, kmb_ref, vmb_ref,
                    sb16_ref, mb16_ref, qmb_ref, u_ref, *maybe_ksum_ref, d_sb, d_mb, d_ssm):
    r = _dot(x_ref[...].astype(BF16), w_ref[...]) + b_ref[...]
    o = 0
    q_sb = r[:, o:o + d_sb]; o += d_sb
    k_sb = r[:, o:o + d_sb]; o += d_sb
    v_sb = r[:, o:o + d_sb]; o += d_sb
    q_mb = r[:, o:o + d_mb]; o += d_mb
    k_mb = r[:, o:o + d_mb]; o += d_mb
    v_mb = r[:, o:o + d_mb]; o += d_mb
    u = r[:, o:o + d_ssm]
    scale = HEAD_DIM ** -0.5
    ksb_ref[...] = k_sb
    vsb_ref[...] = v_sb
    kmb_ref[...] = k_mb
    vmb_ref[...] = v_mb
    sb16_ref[:, 0:d_sb] = (q_sb * scale).astype(BF16)
    sb16_ref[:, d_sb:2 * d_sb] = k_sb.astype(BF16)
    sb16_ref[:, 2 * d_sb:3 * d_sb] = v_sb.astype(BF16)
    mb16_ref[:, 0:d_mb] = (q_mb * scale).astype(BF16)
    mb16_ref[:, d_mb:2 * d_mb] = k_mb.astype(BF16)
    mb16_ref[:, 2 * d_mb:3 * d_mb] = v_mb.astype(BF16)
    qmb_ref[...] = q_mb
    u_ref[...] = u
    if maybe_ksum_ref:
        ksum_ref, = maybe_ksum_ref
        tm = k_mb.shape[0]
        for j in range(tm // MOBA_BLOCK):
            ksum_ref[0, j:j + 1, :] = jnp.sum(
                k_mb[j * MOBA_BLOCK:(j + 1) * MOBA_BLOCK, :], axis=0, keepdims=True)


def _in_proj(x, w, b, d_sb, d_mb, d_ssm, tm, with_ksum):
    n, d = x.shape
    d_out = w.shape[1]
    row = lambda width: pl.BlockSpec((tm, width), lambda i: (i, 0))
    widths = (d_sb, d_sb, d_mb, d_mb, 3 * d_sb, 3 * d_mb, d_mb, d_ssm)
    dtypes = (F32, F32, F32, F32, BF16, BF16, F32, F32)
    out_specs = [row(wd) for wd in widths]
    out_shape = [jax.ShapeDtypeStruct((n, wd), dt) for wd, dt in zip(widths, dtypes)]
    if with_ksum:
        nblk = tm // MOBA_BLOCK
        out_specs.append(pl.BlockSpec((1, nblk, d_mb), lambda i: (i, 0, 0)))
        out_shape.append(jax.ShapeDtypeStruct((n // tm, nblk, d_mb), F32))
    return pl.pallas_call(
        functools.partial(_in_proj_kernel, d_sb=d_sb, d_mb=d_mb, d_ssm=d_ssm),
        grid=(n // tm,),
        in_specs=[row(d), _resident((d, d_out)), _resident((1, d_out))],
        out_specs=out_specs,
        out_shape=out_shape,
        compiler_params=_cparams(("parallel",)),
        name="in_proj",
    )(x, w, b)


def _sb_attn_kernel(q_ref, k_ref, v_ref, o_ref, *, tile):
    qi = pl.program_id(2)
    q = q_ref[...]
    lane_head = lax.broadcasted_iota(jnp.int32, (tile, LANES), 1) // HEAD_DIM
    row = lax.broadcasted_iota(jnp.int32, (tile, tile), 0)
    col = lax.broadcasted_iota(jnp.int32, (tile, tile), 1)
    tri = (row >= col).astype(BF16)

    def head_out(h):
        qh = jnp.where(lane_head == h, q, jnp.zeros_like(q))

        def cond(st):
            kt, carry, _ = st
            return jnp.logical_and(kt >= 0, jnp.max(carry) > F32_EXP_ZERO)

        def body(st):
            kt, carry, acc = st
            start = pl.multiple_of(kt * tile, tile)
            k = k_ref[pl.ds(start, tile), :]
            v = v_ref[pl.ds(start, tile), :]
            z = _dot_nt(qh, k)
            valid = jnp.logical_or(kt < qi, col < row)
            lk = jnp.where(valid, _neg_softplus(z), 0.0)
            hi, lo = _split_bf16(lk)
            rcs = _dot(hi, tri) + _dot(lo, tri)
            w = jnp.where(valid, jnp.exp(z + carry + rcs), 0.0)
            acc = acc + _dot(w.astype(BF16), v)
            return kt - 1, carry + rcs[:, 0:1], acc

        init = (qi, jnp.zeros((tile, 1), F32), jnp.zeros((tile, LANES), F32))
        return lax.while_loop(cond, body, init)[2]

    out = jnp.where(lane_head == 0, head_out(0), head_out(1))
    o_ref[...] = out.astype(o_ref.dtype)


def _sb_attn(sb16, bsz, t, n_heads):
    tile = min(256, t)
    d_sb = n_heads * HEAD_DIM
    nlb = d_sb // LANES
    x = sb16.reshape(bsz, t, 3 * d_sb)
    return pl.pallas_call(
        functools.partial(_sb_attn_kernel, tile=tile),
        grid=(bsz, nlb, t // tile),
        in_specs=[pl.BlockSpec((None, tile, LANES), lambda b, p, i: (b, i, p)),
                  pl.BlockSpec((None, t, LANES), lambda b, p, i: (b, 0, nlb + p)),
                  pl.BlockSpec((None, t, LANES), lambda b, p, i: (b, 0, 2 * nlb + p))],
        out_specs=pl.BlockSpec((None, tile, LANES), lambda b, p, i: (b, i, p)),
        out_shape=jax.ShapeDtypeStruct((bsz, t, d_sb), BF16),
        compiler_params=_cparams(("parallel", "parallel", "arbitrary")),
        name="sb_attn",
    )(x, x, x)


def _moba_select_bias(score, blk, n_blocks):
    sel = jnp.zeros(score.shape, jnp.bool_)
    s = score
    for _ in range(min(MOBA_TOPK, n_blocks)):
        m = jnp.max(s, axis=0, keepdims=True)
        idx = jnp.min(jnp.where(s == m, blk, n_blocks), axis=0, keepdims=True)
        pick = blk == idx
        sel = jnp.logical_or(sel, jnp.logical_and(pick, jnp.abs(m) < jnp.inf))
        s = jnp.where(pick, -jnp.inf, s)
    return jnp.where(sel, 0.0, NEG_BIG)


def _moba_attn_kernel(q32_ref, k_ref, vt_ref, ksum_ref, o_ref, bias_ref,
                      za_ref, zb_ref, pa_ref, pb_ref, *, n_blocks):
    qi = pl.program_id(2)
    tile = MOBA_BLOCK
    step = 2 * tile
    qt = q32_ref[...].T
    row_head = lax.broadcasted_iota(jnp.int32, (LANES, tile), 0) // HEAD_DIM
    blk = lax.broadcasted_iota(jnp.int32, (n_blocks, tile), 0)
    ksum = ksum_ref[...]
    scale = HEAD_DIM ** -0.5

    qts = []
    for h in range(HEADS_PER_VREG):
        qh = jnp.where(row_head == h, qt, 0.0)
        score = jnp.dot(ksum, qh, precision=_HI, preferred_element_type=F32) * (1.0 / MOBA_BLOCK)
        score = jnp.where(blk < qi, score, -jnp.inf)
        bias = _moba_select_bias(score, blk, n_blocks)
        for jp in range(n_blocks // 2):
            bias_ref[h, jp, 0:2, :] = bias[2 * jp:2 * jp + 2]
        qts.append((qh * scale).astype(BF16))

    key = lax.broadcasted_iota(jnp.int32, (tile, tile), 0)
    qry = lax.broadcasted_iota(jnp.int32, (tile, tile), 1)
    start = pl.multiple_of(qi * tile, tile)
    k_own = k_ref[pl.ds(start, tile), :]
    ms, ls, accs = [], [], []
    for h in range(HEADS_PER_VREG):
        zt = jnp.where(key <= qry, _dot(k_own, qts[h]), NEG_BIG)
        m = jnp.max(zt, axis=0, keepdims=True)
        p = jnp.exp(zt - m)
        ms.append(m)
        ls.append(jnp.sum(p, axis=0, keepdims=True))
        accs.append(_dot(vt_ref[h * HEAD_DIM:(h + 1) * HEAD_DIM, pl.ds(start, tile)], p.astype(BF16)))

    last_pair = n_blocks // 2 - 1
    heads = range(HEADS_PER_VREG)

    def put_logits(z_ref, j):
        start = pl.multiple_of(jnp.minimum(j, last_pair) * step, step)
        k2 = k_ref[pl.ds(start, step), :]
        for h in heads:
            z_ref[h] = _dot(k2, qts[h])

    def weighted_values(p_ref, j):
        start = pl.multiple_of(jnp.clip(j, 0, last_pair) * step, step)
        return [_dot(vt_ref[h * HEAD_DIM:(h + 1) * HEAD_DIM, pl.ds(start, step)], p_ref[h])
                for h in heads]

    def trip(j, st, z_cur, z_next, p_cur, p_prev):
        ms, ls, accs, corrs = st
        put_logits(z_next, j + 1)
        pvs = weighted_values(p_prev, j - 1)
        out_m, out_l, out_acc, out_corr = [], [], [], []
        for h in heads:
            bias = bias_ref[h, jnp.minimum(j, last_pair), 0:2, :]
            zt = jnp.concatenate([z_cur[h, :tile] + bias[0:1], z_cur[h, tile:] + bias[1:2]], axis=0)
            m_new = jnp.maximum(ms[h], jnp.max(zt, axis=0, keepdims=True))
            p = jnp.exp(zt - m_new)
            p_cur[h] = p.astype(BF16)
            corr = jnp.exp(ms[h] - m_new)
            out_m.append(m_new)
            out_l.append(ls[h] * corr + jnp.sum(p, axis=0, keepdims=True))
            out_acc.append(accs[h] * corrs[h] + pvs[h])
            out_corr.append(corr)
        return out_m, out_l, out_acc, out_corr

    def two_trips(i, st):
        st = trip(2 * i, st, za_ref, zb_ref, pa_ref, pb_ref)
        return trip(2 * i + 1, st, zb_ref, za_ref, pb_ref, pa_ref)

    put_logits(za_ref, 0)
    pb_ref[...] = jnp.zeros_like(pb_ref)
    ones = [jnp.ones((1, tile), F32) for _ in heads]
    n_trips = (qi + 1) // 2
    ms, ls, accs, corrs = lax.fori_loop(0, (n_trips + 1) // 2, two_trips, (ms, ls, accs, ones))
    pvs = weighted_values(pb_ref, 2 * ((n_trips + 1) // 2) - 1)
    out_t = jnp.concatenate([(accs[h] * corrs[h] + pvs[h]) / ls[h] for h in heads], axis=0)
    o_ref[...] = out_t.T.astype(o_ref.dtype)


def _moba_attn(mb16, q_mb32, ksum, bsz, t, n_heads):
    tile = MOBA_BLOCK
    d_mb = n_heads * HEAD_DIM
    nlb = d_mb // LANES
    n_blocks = t // tile
    assert n_blocks % 2 == 0
    x = mb16.reshape(bsz, t, 3 * d_mb)
    vt = jnp.swapaxes(x[:, :, 2 * d_mb:], 1, 2)
    q32 = q_mb32.reshape(bsz, t, d_mb)
    return pl.pallas_call(
        functools.partial(_moba_attn_kernel, n_blocks=n_blocks),
        grid=(bsz, nlb, n_blocks),
        in_specs=[pl.BlockSpec((None, tile, LANES), lambda b, p, i: (b, i, p)),
                  pl.BlockSpec((None, t, LANES), lambda b, p, i: (b, 0, nlb + p)),
                  pl.BlockSpec((None, LANES, t), lambda b, p, i: (b, p, 0)),
                  pl.BlockSpec((None, n_blocks, LANES), lambda b, p, i: (b, 0, p))],
        out_specs=pl.BlockSpec((None, tile, LANES), lambda b, p, i: (b, i, p)),
        out_shape=jax.ShapeDtypeStruct((bsz, t, d_mb), BF16),
        scratch_shapes=[pltpu.VMEM((HEADS_PER_VREG, n_blocks // 2, SUBLANES, tile), F32)]
        + [pltpu.VMEM((HEADS_PER_VREG, 2 * tile, tile), F32)] * 2
        + [pltpu.VMEM((HEADS_PER_VREG, 2 * tile, tile), BF16)] * 2,
        compiler_params=_cparams(("parallel", "parallel", "arbitrary")),
        name="moba_attn",
    )(q32, x, vt, ksum)


def _ssm_tables(a_re, a_im, log_dt, b_re, b_im, c_re, c_im, ssm_d):
    ng, ns = a_re.shape
    nc = SSM_GROUP_CH
    L = SSM_CHUNK
    a_re, a_im = a_re.astype(F32), a_im.astype(F32)
    dt = jnp.exp(log_dt.astype(F32))[:, None]
    tau = jnp.arange(L + 1, dtype=F32)[:, None, None]
    mag = jnp.exp(tau * (a_re * dt))
    ang = tau * (a_im * dt)
    pw_re, pw_im = mag * jnp.cos(ang), mag * jnp.sin(ang)
    lam_re, lam_im = pw_re[1], pw_im[1]
    den = a_re * a_re + a_im * a_im
    num_re, num_im = lam_re - 1.0, lam_im
    coef_re = (num_re * a_re + num_im * a_im) / den
    coef_im = (num_im * a_re - num_re * a_im) / den
    b_re, b_im = b_re.astype(F32), b_im.astype(F32)
    bb_re = coef_re[..., None] * b_re - coef_im[..., None] * b_im
    bb_im = coef_re[..., None] * b_im + coef_im[..., None] * b_re
    c_re, c_im = c_re.astype(F32), c_im.astype(F32)
    x_re = pw_re[..., None] * bb_re - pw_im[..., None] * bb_im
    x_im = pw_re[..., None] * bb_im + pw_im[..., None] * bb_re
    kern = (jnp.einsum('gcp,tgpd->tgcd', c_re, x_re, precision=_HI)
            - jnp.einsum('gcp,tgpd->tgcd', c_im, x_im, precision=_HI))
    s_idx = jnp.arange(L)[:, None]
    t_idx = jnp.arange(L)[None, :]
    delta = t_idx - s_idx
    toep = jnp.where((delta >= 0)[:, :, None, None, None], kern[jnp.clip(delta, 0, L)], 0.0)
    m_intra = jnp.transpose(toep, (2, 0, 4, 1, 3)).reshape(ng, L * nc, L * nc)
    ws_re = jnp.transpose(x_re[L - 1 - jnp.arange(L)], (1, 0, 3, 2)).reshape(ng, L * nc, ns)
    ws_im = jnp.transpose(x_im[L - 1 - jnp.arange(L)], (1, 0, 3, 2)).reshape(ng, L * nc, ns)
    w_state = jnp.concatenate([ws_re, ws_im, ws_im, ws_re], axis=-1)
    cl_re = c_re[None] * jnp.transpose(pw_re[1:], (0, 1, 2))[:, :, None, :] \
        - c_im[None] * pw_im[1:][:, :, None, :]
    cl_im = c_re[None] * pw_im[1:][:, :, None, :] + c_im[None] * pw_re[1:][:, :, None, :]
    wh_re = jnp.transpose(cl_re, (1, 3, 0, 2)).reshape(ng, ns, L * nc)
    wh_im = jnp.transpose(cl_im, (1, 3, 0, 2)).reshape(ng, ns, L * nc)
    w_carry = jnp.concatenate([wh_re, -wh_im], axis=1)
    lr, li = pw_re[L], pw_im[L]
    dec_a = jnp.concatenate([lr, lr], axis=-1).reshape(1, 2 * ng * ns)
    dec_p = jnp.concatenate([-li, li], axis=-1).reshape(1, 2 * ng * ns)
    d_tile = jnp.tile(ssm_d.astype(F32).reshape(ng, 1, nc), (1, 1, L))
    return dict(m_intra=m_intra.astype(BF16), w_state=w_state.astype(BF16),
                w_carry=w_carry.astype(BF16), dec_a=dec_a, dec_p=dec_p, d_tile=d_tile,
                lam_re=lam_re, lam_im=lam_im, bb_re=bb_re, bb_im=bb_im)


def _ssm_prompt_kernel(u_ref, mi_ref, ws_ref, wc_ref, da_ref, dp_ref, dt_ref,
                       y_ref, hfin_ref, sp_ref, sq_ref, hprev_ref, p_ref, q_ref,
                       *, n_groups, n_state, bp, n_chunks):
    step = pl.program_id(0)
    sw = 2 * n_state

    @pl.when(step == 0)
    def _():
        p_ref[...] = jnp.zeros_like(p_ref)
        q_ref[...] = jnp.zeros_like(q_ref)

    for g in range(n_groups):
        ug = u_ref[g]
        ugb = ug.astype(BF16)
        y_ref[g] = _dot(ugb, mi_ref[g]) + ug * dt_ref[g]
        s4 = _dot(ugb, ws_ref[g])
        sp_ref[:, g * sw:(g + 1) * sw] = s4[:, :sw]
        sq_ref[:, g * sw:(g + 1) * sw] = s4[:, sw:]

    dec_a = da_ref[...]
    dec_p = dp_ref[...]

    def scan(c, st):
        p, q = st
        r0 = pl.multiple_of(c * bp, bp)
        hprev_ref[pl.ds(r0, bp), :] = p
        p_new = dec_a * p + dec_p * q + sp_ref[pl.ds(r0, bp), :]
        q_new = dec_a * q - dec_p * p + sq_ref[pl.ds(r0, bp), :]
        return p_new, q_new

    p, q = lax.fori_loop(0, n_chunks, scan, (p_ref[...], q_ref[...]))
    p_ref[...] = p
    q_ref[...] = q
    hfin_ref[...] = p

    for g in range(n_groups):
        hg = hprev_ref[:, g * sw:(g + 1) * sw].astype(BF16)
        y_ref[g] = y_ref[g] + _dot(hg, wc_ref[g])


def _ssm_prompt(u, tabs, bsz, t, n_groups, n_state):
    L, nc = SSM_CHUNK, SSM_GROUP_CH
    bp = -(-bsz // SUBLANES) * SUBLANES
    n_chunk_total = t // L
    n_chunks = min(32, n_chunk_total)
    rows = n_chunks * bp
    width = L * nc
    ul = jnp.transpose(u.reshape(bsz, n_chunk_total, L, n_groups, nc), (3, 1, 0, 2, 4))
    ul = jnp.pad(ul, ((0, 0), (0, 0), (0, bp - bsz), (0, 0), (0, 0)))
    ul = ul.reshape(n_groups, n_chunk_total * bp, width)
    sl = 2 * n_state * n_groups
    y, hfin = pl.pallas_call(
        functools.partial(_ssm_prompt_kernel, n_groups=n_groups, n_state=n_state, bp=bp,
                          n_chunks=n_chunks),
        grid=(n_chunk_total // n_chunks,),
        in_specs=[pl.BlockSpec((n_groups, rows, width), lambda i: (0, i, 0)),
                  _resident((n_groups, width, width)),
                  _resident((n_groups, width, 4 * n_state)),
                  _resident((n_groups, 2 * n_state, width)),
                  _resident((1, sl)), _resident((1, sl)),
                  _resident((n_groups, 1, width))],
        out_specs=[pl.BlockSpec((n_groups, rows, width), lambda i: (0, i, 0)),
                   pl.BlockSpec((bp, sl), lambda i: (0, 0))],
        out_shape=[jax.ShapeDtypeStruct((n_groups, n_chunk_total * bp, width), F32),
                   jax.ShapeDtypeStruct((bp, sl), F32)],
        scratch_shapes=[pltpu.VMEM((rows, sl), F32), pltpu.VMEM((rows, sl), F32),
                        pltpu.VMEM((rows, sl), F32),
                        pltpu.VMEM((bp, sl), F32), pltpu.VMEM((bp, sl), F32)],
        compiler_params=_cparams(("arbitrary",)),
        name="ssm_prompt",
    )(ul, tabs['m_intra'], tabs['w_state'], tabs['w_carry'], tabs['dec_a'], tabs['dec_p'],
      tabs['d_tile'])
    y = y.reshape(n_groups, n_chunk_total, bp, L, nc)[:, :, :bsz]
    y = jnp.transpose(y, (2, 1, 3, 0, 4)).reshape(bsz * t, n_groups * nc)
    h = hfin[:bsz].reshape(bsz, n_groups, 2, n_state)
    return y, h[:, :, 0], h[:, :, 1]


def _ssm_step_kernel(u_ref, hr_ref, hi_ref, wbr_ref, wbi_ref, lr_ref, li_ref,
                     wcr_ref, wci_ref, d_ref, y_ref, hro_ref, hio_ref):
    u = u_ref[...]
    dot = lambda a, b: jnp.dot(a, b, precision=_HI, preferred_element_type=F32)
    h0r, h0i = hr_ref[...], hi_ref[...]
    lr, li = lr_ref[...], li_ref[...]
    hr = dot(u, wbr_ref[...]) + (lr * h0r - li * h0i)
    hi = dot(u, wbi_ref[...]) + (lr * h0i + li * h0r)
    hro_ref[...] = hr
    hio_ref[...] = hi
    y_ref[...] = dot(hr, wcr_ref[...]) - dot(hi, wci_ref[...]) + d_ref[...] * u


def _ssm_step(u, h0_re, h0_im, tabs, c_re, c_im, ssm_d):
    bsz, ng, ns = h0_re.shape
    nc = SSM_GROUP_CH
    eye = jnp.eye(ng, dtype=F32)
    blockdiag = lambda w: jnp.einsum('gab,gh->gahb', w, eye).reshape(ng * w.shape[1], ng * w.shape[2])
    wb_re = blockdiag(jnp.swapaxes(tabs['bb_re'], 1, 2))
    wb_im = blockdiag(jnp.swapaxes(tabs['bb_im'], 1, 2))
    wc_re = blockdiag(jnp.swapaxes(c_re.astype(F32), 1, 2))
    wc_im = blockdiag(jnp.swapaxes(c_im.astype(F32), 1, 2))
    flat = lambda a: a.astype(F32).reshape(1, ng * ns)
    y, hr, hi = pl.pallas_call(
        _ssm_step_kernel,
        out_shape=[jax.ShapeDtypeStruct((bsz, ng * nc), F32),
                   jax.ShapeDtypeStruct((bsz, ng * ns), F32),
                   jax.ShapeDtypeStruct((bsz, ng * ns), F32)],
        compiler_params=pltpu.CompilerParams(vmem_limit_bytes=VMEM_LIMIT),
        name="ssm_step",
    )(u, h0_re.astype(F32).reshape(bsz, ng * ns), h0_im.astype(F32).reshape(bsz, ng * ns),
      wb_re, wb_im, flat(tabs['lam_re']), flat(tabs['lam_im']), wc_re, wc_im,
      ssm_d.astype(F32).reshape(1, ng * nc))
    return y, hr.reshape(bsz, ng, ns), hi.reshape(bsz, ng, ns)


def _merge_kernel(x_ref, osb_ref, omb_ref, yssm_ref, wg_ref, bg_ref, wglu_ref,
                  wbsb_ref, wbmb_ref, wbssm_ref, wout_ref, g_ref, b_ref, o_ref, *, alpha):
    x = x_ref[...]
    d = x.shape[1]
    gates = jax.nn.sigmoid(_dot(x.astype(BF16), wg_ref[...]) + bg_ref[...])
    glu = _dot(yssm_ref[...].astype(BF16), wglu_ref[...])
    half = glu.shape[1] // 2
    o_ssm = glu[:, :half] * jax.nn.sigmoid(glu[:, half:])
    merged = (gates[:, 0:d] * _dot(osb_ref[...], wbsb_ref[...])
              + gates[:, d:2 * d] * _dot(omb_ref[...], wbmb_ref[...])
              + gates[:, 2 * d:3 * d] * _dot(o_ssm.astype(BF16), wbssm_ref[...]))
    mix = _dot(merged.astype(BF16), wout_ref[...])
    o_ref[...] = _layer_norm(alpha * x + mix, g_ref[...], b_ref[...])


def _merge(x, o_sb, o_mb, y_ssm, lw, alpha, tm):
    n, d = x.shape
    row = lambda a: pl.BlockSpec((tm, a.shape[1]), lambda i: (i, 0))
    acts = (x, o_sb, o_mb, y_ssm)
    weights = (lw['w_gate'], lw['b_gate'], lw['w_glu'], lw['w_br_sb'], lw['w_br_mb'],
               lw['w_br_ssm'], lw['w_out'], lw['ln2_g'], lw['ln2_b'])
    return pl.pallas_call(
        functools.partial(_merge_kernel, alpha=alpha),
        grid=(n // tm,),
        in_specs=[row(a) for a in acts] + [_resident(w.shape) for w in weights],
        out_specs=pl.BlockSpec((tm, d), lambda i: (i, 0)),
        out_shape=jax.ShapeDtypeStruct((n, d), F32),
        compiler_params=_cparams(("parallel",)),
        name="merge",
    )(*acts, *weights)


def _pages_keys_minor(cache):
    return jnp.transpose(cache, (0, 1, 3, 4, 2))


def _sb_decode_kernel(pt_ref, q_ref, *refs, n_heads, pages_per_step):
    k_refs = refs[:pages_per_step]
    v_refs = refs[pages_per_step:2 * pages_per_step]
    o_ref = refs[2 * pages_per_step]
    carry_ref, acc_ref = refs[2 * pages_per_step + 1:]
    j = pl.program_id(1)
    page = k_refs[0].shape[-1]
    assert n_heads <= SUBLANES

    @pl.when(j == 0)
    def _():
        carry_ref[...] = jnp.zeros_like(carry_ref)
        acc_ref[...] = jnp.zeros_like(acc_ref)

    q = q_ref[...] * (HEAD_DIM ** -0.5)
    row = lax.broadcasted_iota(jnp.int32, (page, page), 0)
    col = lax.broadcasted_iota(jnp.int32, (page, page), 1)
    tri = (row >= col).astype(BF16)
    head_rows = lax.broadcasted_iota(jnp.int32, (SUBLANES, 1), 0) < n_heads
    zero_rows = jnp.zeros((SUBLANES - n_heads, page), F32)

    for i in range(pages_per_step):
        carry = carry_ref[...]
        alive = jnp.max(jnp.where(head_rows, carry, -jnp.inf)) > F32_EXP_ZERO

        @pl.when(alive)
        def _():
            z = jnp.concatenate(
                [jnp.sum(k_refs[i][h] * q[h * HEAD_DIM:(h + 1) * HEAD_DIM], axis=0, keepdims=True)
                 for h in range(n_heads)] + [zero_rows], axis=0)
            lk = _neg_softplus(z)
            hi, lo = _split_bf16(lk)
            rcs = _dot(hi, tri) + _dot(lo, tri)
            w = jnp.exp(z + carry + rcs)
            for h in range(n_heads):
                acc_ref[h * HEAD_DIM:(h + 1) * HEAD_DIM, :] += jnp.sum(
                    v_refs[i][h] * w[h:h + 1, :], axis=-1, keepdims=True)
            carry_ref[...] = carry + rcs[:, 0:1]

    @pl.when(j == pl.num_programs(1) - 1)
    def _():
        o_ref[...] = acc_ref[...]


def _sb_decode(q, cache_k, cache_v, layer, page_table, n_heads):
    bsz, n_pages = page_table.shape
    page = cache_k.shape[2]
    d = n_heads * HEAD_DIM
    pps = math.gcd(n_pages, 8)
    ck, cv = _pages_keys_minor(cache_k), _pages_keys_minor(cache_v)

    def page_spec(i):
        return pl.BlockSpec((None, None, n_heads, HEAD_DIM, page),
                            lambda b, j, pt: (layer, pt[b, n_pages - 1 - (j * pps + i)], 0, 0, 0))

    col = pl.BlockSpec((None, d, 1), lambda b, j, pt: (b, 0, 0))
    grid_spec = pltpu.PrefetchScalarGridSpec(
        num_scalar_prefetch=1,
        grid=(bsz, n_pages // pps),
        in_specs=[col] + [page_spec(i) for i in range(pps)] * 2,
        out_specs=col,
        scratch_shapes=[pltpu.VMEM((SUBLANES, 1), F32), pltpu.VMEM((d, 1), F32)],
    )
    out = pl.pallas_call(
        functools.partial(_sb_decode_kernel, n_heads=n_heads, pages_per_step=pps),
        grid_spec=grid_spec,
        out_shape=jax.ShapeDtypeStruct((bsz, d, 1), F32),
        compiler_params=_cparams(("parallel", "arbitrary")),
        name="sb_decode",
    )(page_table, q.reshape(bsz, d, 1), *([ck] * pps), *([cv] * pps))
    return out.reshape(bsz, d).astype(BF16)


def _moba_ksum_kernel(pt_ref, *refs, pages_per_step, pages_per_block):
    k_refs = refs[:pages_per_step]
    o_ref = refs[pages_per_step]
    j = pl.program_id(1)
    bps = pages_per_step // pages_per_block

    @pl.when(j == 0)
    def _():
        o_ref[...] = jnp.zeros_like(o_ref)

    lane = lax.broadcasted_iota(jnp.int32, o_ref.shape, 1)
    out = o_ref[...]
    for m in range(bps):
        s = k_refs[m * pages_per_block][...]
        for r in range(1, pages_per_block):
            s = s + k_refs[m * pages_per_block + r][...]
        col = jnp.sum(s.reshape(-1, s.shape[-1]), axis=-1, keepdims=True)
        out = jnp.where(lane == j * bps + m, col, out)
    o_ref[...] = out


def _moba_ksum(cache_k, layer, page_table, n_heads):
    bsz, n_pages = page_table.shape
    page = cache_k.shape[2]
    d = n_heads * HEAD_DIM
    ppb = MOBA_BLOCK // page
    n_blocks = n_pages // ppb
    pps = math.gcd(n_blocks, 8) * ppb
    ck = _pages_keys_minor(cache_k)

    def page_spec(i):
        return pl.BlockSpec((None, None, n_heads, HEAD_DIM, page),
                            lambda b, j, pt: (layer, pt[b, j * pps + i], 0, 0, 0))

    grid_spec = pltpu.PrefetchScalarGridSpec(
        num_scalar_prefetch=1,
        grid=(bsz, n_pages // pps),
        in_specs=[page_spec(i) for i in range(pps)],
        out_specs=pl.BlockSpec((None, d, n_blocks), lambda b, j, pt: (b, 0, 0)),
    )
    return pl.pallas_call(
        functools.partial(_moba_ksum_kernel, pages_per_step=pps, pages_per_block=ppb),
        grid_spec=grid_spec,
        out_shape=jax.ShapeDtypeStruct((bsz, d, n_blocks), F32),
        compiler_params=_cparams(("parallel", "arbitrary")),
        name="moba_ksum",
    )(page_table, *([ck] * pps))


def _moba_topk_kernel(q_ref, ksum_ref, idx_ref, ok_ref, *, n_heads, n_blocks):
    prod = ksum_ref[...] * q_ref[...]
    score = jnp.sum(prod.reshape(n_heads, HEAD_DIM, n_blocks), axis=1) * (1.0 / MOBA_BLOCK)
    blk = lax.broadcasted_iota(jnp.int32, (n_heads, n_blocks), 1)
    lane = lax.broadcasted_iota(jnp.int32, (n_heads, LANES), 1)
    s = score
    idx_out = jnp.zeros((n_heads, LANES), jnp.int32)
    ok_out = jnp.zeros((n_heads, LANES), jnp.int32)
    for r in range(MOBA_TOPK):
        m = jnp.max(s, axis=-1, keepdims=True)
        idx = jnp.min(jnp.where(s == m, blk, n_blocks), axis=-1, keepdims=True)
        idx_out = jnp.where(lane == r, jnp.minimum(idx, n_blocks - 1), idx_out)
        ok_out = jnp.where(lane == r, (jnp.abs(m) < jnp.inf).astype(jnp.int32), ok_out)
        s = jnp.where(blk == idx, -jnp.inf, s)
    idx_ref[...] = idx_out
    ok_ref[...] = ok_out


def _moba_topk(q, ksum_t, n_heads):
    bsz, d, n_blocks = ksum_t.shape
    assert n_blocks >= MOBA_TOPK
    idx, ok = pl.pallas_call(
        functools.partial(_moba_topk_kernel, n_heads=n_heads, n_blocks=n_blocks),
        grid=(bsz,),
        in_specs=[pl.BlockSpec((None, d, 1), lambda b: (b, 0, 0)),
                  pl.BlockSpec((None, d, n_blocks), lambda b: (b, 0, 0))],
        out_specs=[pl.BlockSpec((None, n_heads, LANES), lambda b: (b, 0, 0))] * 2,
        out_shape=[jax.ShapeDtypeStruct((bsz, n_heads, LANES), jnp.int32)] * 2,
        compiler_params=_cparams(("parallel",)),
        name="moba_topk",
    )(q.reshape(bsz, d, 1), ksum_t)
    return idx[:, :, :MOBA_TOPK], ok[:, :, :MOBA_TOPK]


def _moba_decode_kernel(pt_ref, idx_ref, ok_ref, q_ref, kn_ref, vn_ref, *refs, n_tiles):
    k_refs = refs[:n_tiles]
    v_refs = refs[n_tiles:2 * n_tiles]
    o_ref = refs[2 * n_tiles]
    b = pl.program_id(0)
    h = pl.program_id(1)
    n_heads = pl.num_programs(1)
    ppb = n_tiles // MOBA_TOPK
    q = q_ref[...] * (HEAD_DIM ** -0.5)
    z_own = jnp.sum(q * kn_ref[...], axis=0, keepdims=True)
    zs = []
    for t in range(n_tiles):
        ok = ok_ref[(b * n_heads + h) * MOBA_TOPK + t // ppb] > 0
        z = jnp.sum(k_refs[t][...] * q, axis=0, keepdims=True)
        zs.append(jnp.where(ok, z, NEG_BIG))
    m = z_own
    for z in zs:
        m = jnp.maximum(m, jnp.max(z, axis=-1, keepdims=True))
    p_own = jnp.exp(z_own - m)
    l = p_own
    acc = p_own * vn_ref[...]
    for t in range(n_tiles):
        pw = jnp.exp(zs[t] - m)
        l = l + jnp.sum(pw, axis=-1, keepdims=True)
        acc = acc + jnp.sum(v_refs[t][...] * pw, axis=-1, keepdims=True)
    o_ref[...] = acc / l


def _moba_decode(q, k_new, v_new, idx, ok, cache_k, cache_v, layer, page_table, n_heads):
    bsz, n_pages = page_table.shape
    page = cache_k.shape[2]
    d = n_heads * HEAD_DIM
    ppb = MOBA_BLOCK // page
    n_tiles = MOBA_TOPK * ppb
    ck, cv = _pages_keys_minor(cache_k), _pages_keys_minor(cache_v)

    def tile_spec(t):
        def index_map(b, h, pt, idx, ok):
            blk = idx[(b * n_heads + h) * MOBA_TOPK + t // ppb]
            return (layer, pt[b, blk * ppb + t % ppb], h, 0, 0)
        return pl.BlockSpec((None, None, None, HEAD_DIM, page), index_map)

    tiles = [tile_spec(t) for t in range(n_tiles)]
    col = pl.BlockSpec((None, HEAD_DIM, 1), lambda b, h, pt, idx, ok: (b, h, 0))
    grid_spec = pltpu.PrefetchScalarGridSpec(
        num_scalar_prefetch=3,
        grid=(bsz, n_heads),
        in_specs=[col, col, col] + tiles * 2,
        out_specs=col,
    )
    out = pl.pallas_call(
        functools.partial(_moba_decode_kernel, n_tiles=n_tiles),
        grid_spec=grid_spec,
        out_shape=jax.ShapeDtypeStruct((bsz, d, 1), F32),
        compiler_params=_cparams(("parallel", "parallel")),
        name="moba_decode",
    )(page_table, idx.reshape(-1), ok.reshape(-1), q.reshape(bsz, d, 1), k_new.reshape(bsz, d, 1),
      v_new.reshape(bsz, d, 1), *([ck] * n_tiles), *([cv] * n_tiles))
    return out.reshape(bsz, d).astype(BF16)


def _row_tile(n):
    for tm in (512, 256, 128, 64, 32, 16, 8):
        if n % tm == 0:
            return tm
    return n


def _layer_weights(w, l, d_model, d_sb, d_mb, d_ssm):
    n_qkvu = 3 * d_sb + 3 * d_mb + d_ssm
    bf = lambda a: a.astype(BF16)
    vec = lambda a: a.astype(F32).reshape(1, -1)
    lw = dict(
        w_ffn1_up=bf(w['w_ffn1_up'][l]), w_ffn1_down=bf(w['w_ffn1_down'][l]),
        w_ffn2_up=bf(w['w_ffn2_up'][l]), w_ffn2_down=bf(w['w_ffn2_down'][l]),
        w_qkvu=bf(w['w_in'][l][:, :n_qkvu]), b_qkvu=vec(w['b_in'][l][:n_qkvu]),
        w_gate=bf(w['w_in'][l][:, n_qkvu:]), b_gate=vec(w['b_in'][l][n_qkvu:]),
        w_glu=bf(w['w_glu'][l]), w_br_sb=bf(w['w_br_sb'][l]), w_br_mb=bf(w['w_br_mb'][l]),
        w_br_ssm=bf(w['w_br_ssm'][l]), w_out=bf(w['w_out'][l]),
    )
    for name in ('ln1_g', 'ln1_b', 'ln2_g', 'ln2_b', 'ln3_g', 'ln3_b'):
        lw[name] = vec(w[name][l])
    return lw


def kernel(x_prompt, x_sample, cache_k_sb, cache_v_sb, cache_k_mb, cache_v_mb, state_ssm_re, state_ssm_im, page_table, ln1_g, ln1_b, w_ffn1_up, w_ffn1_down, w_in, b_in, ssm_a_re, ssm_a_im, ssm_log_dt, ssm_b_re, ssm_b_im, ssm_c_re, ssm_c_im, ssm_d, w_glu, w_br_sb, w_br_mb, w_br_ssm, w_out, ln2_g, ln2_b, w_ffn2_up, w_ffn2_down, ln3_g, ln3_b):
    weights = dict(ln1_g=ln1_g, ln1_b=ln1_b, w_ffn1_up=w_ffn1_up, w_ffn1_down=w_ffn1_down,
                   w_in=w_in, b_in=b_in, w_glu=w_glu, w_br_sb=w_br_sb, w_br_mb=w_br_mb,
                   w_br_ssm=w_br_ssm, w_out=w_out, ln2_g=ln2_g, ln2_b=ln2_b,
                   w_ffn2_up=w_ffn2_up, w_ffn2_down=w_ffn2_down, ln3_g=ln3_g, ln3_b=ln3_b)
    depth = w_in.shape[0]
    bsz, t, d_model = x_prompt.shape
    dbsz, dec_t, _ = x_sample.shape
    h_sb, h_mb = cache_k_sb.shape[3], cache_k_mb.shape[3]
    n_groups, n_state = ssm_a_re.shape[1:]
    d_sb, d_mb, d_ssm = h_sb * HEAD_DIM, h_mb * HEAD_DIM, n_groups * SSM_GROUP_CH
    page = cache_k_sb.shape[2]
    past_len = page_table.shape[1] * page
    assert dec_t == 1 and t % MOBA_BLOCK == 0 and t % SSM_CHUNK == 0
    assert MOBA_BLOCK % page == 0 and past_len % MOBA_BLOCK == 0
    assert d_sb % LANES == 0 and d_mb % LANES == 0
    alpha = float((2 * depth) ** 0.25)

    n_p = bsz * t
    tm_p = _row_tile(n_p)
    if tm_p % MOBA_BLOCK:
        tm_p = MOBA_BLOCK
    tm_s = _row_tile(dbsz)

    y_p = x_prompt.reshape(n_p, d_model)
    y_s = x_sample.reshape(dbsz, d_model)
    prompt_states, sample_states = [], []
    for l in range(depth):
        lw = _layer_weights(weights, l, d_model, d_sb, d_mb, d_ssm)
        tabs = _ssm_tables(ssm_a_re[l], ssm_a_im[l], ssm_log_dt[l], ssm_b_re[l], ssm_b_im[l],
                           ssm_c_re[l], ssm_c_im[l], ssm_d[l])

        x1 = _ffn_ln(y_p, lw['w_ffn1_up'], lw['w_ffn1_down'], lw['ln1_g'], lw['ln1_b'], alpha, tm_p)
        (k_sb, v_sb, k_mb, v_mb, sb16, mb16, q_mb32, u, ksum) = _in_proj(
            x1, lw['w_qkvu'], lw['b_qkvu'], d_sb, d_mb, d_ssm, tm_p, True)
        o_sb = _sb_attn(sb16, bsz, t, h_sb).reshape(n_p, d_sb)
        o_mb = _moba_attn(mb16, q_mb32, ksum.reshape(bsz, t // MOBA_BLOCK, d_mb),
                          bsz, t, h_mb).reshape(n_p, d_mb)
        y_ssm, h_re, h_im = _ssm_prompt(u, tabs, bsz, t, n_groups, n_state)
        x2 = _merge(x1, o_sb, o_mb, y_ssm, lw, alpha, tm_p)
        y_p = _ffn_ln(x2, lw['w_ffn2_up'], lw['w_ffn2_down'], lw['ln3_g'], lw['ln3_b'], alpha, tm_p)
        prompt_states.append((k_sb.reshape(bsz, t, h_sb, HEAD_DIM), v_sb.reshape(bsz, t, h_sb, HEAD_DIM),
                              k_mb.reshape(bsz, t, h_mb, HEAD_DIM), v_mb.reshape(bsz, t, h_mb, HEAD_DIM),
                              h_re.astype(state_ssm_re.dtype), h_im.astype(state_ssm_im.dtype)))

        s1 = _ffn_ln(y_s, lw['w_ffn1_up'], lw['w_ffn1_down'], lw['ln1_g'], lw['ln1_b'], alpha, tm_s)
        (k_sb, v_sb, k_mb, v_mb, sb16, mb16, q_mb32, u) = _in_proj(
            s1, lw['w_qkvu'], lw['b_qkvu'], d_sb, d_mb, d_ssm, tm_s, False)
        q_sb32 = sb16[:, :d_sb].astype(F32) * (HEAD_DIM ** 0.5)
        o_sb = _sb_decode(q_sb32, cache_k_sb, cache_v_sb, l, page_table, h_sb)
        ksum = _moba_ksum(cache_k_mb, l, page_table, h_mb)
        idx, ok = _moba_topk(q_mb32, ksum, h_mb)
        o_mb = _moba_decode(q_mb32, k_mb, v_mb, idx, ok, cache_k_mb, cache_v_mb, l, page_table, h_mb)
        y_ssm, h_re, h_im = _ssm_step(u, state_ssm_re[l], state_ssm_im[l], tabs,
                                      ssm_c_re[l], ssm_c_im[l], ssm_d[l])
        s2 = _merge(s1, o_sb, o_mb, y_ssm, lw, alpha, tm_s)
        y_s = _ffn_ln(s2, lw['w_ffn2_up'], lw['w_ffn2_down'], lw['ln3_g'], lw['ln3_b'], alpha, tm_s)
        sample_states.append((k_sb.reshape(dbsz, 1, h_sb, HEAD_DIM), v_sb.reshape(dbsz, 1, h_sb, HEAD_DIM),
                              k_mb.reshape(dbsz, 1, h_mb, HEAD_DIM), v_mb.reshape(dbsz, 1, h_mb, HEAD_DIM),
                              h_re.astype(state_ssm_re.dtype), h_im.astype(state_ssm_im.dtype)))

    stack = lambda states: tuple(jnp.stack(group) for group in zip(*states))
    k_sb_p, v_sb_p, k_mb_p, v_mb_p, h_re_p, h_im_p = stack(prompt_states)
    k_sb_s, v_sb_s, k_mb_s, v_mb_s, h_re_s, h_im_s = stack(sample_states)
    return (y_p.reshape(bsz, t, d_model), y_s.reshape(dbsz, dec_t, d_model),
            k_sb_p, v_sb_p, k_mb_p, v_mb_p, h_re_p, h_im_p,
            k_sb_s, v_sb_s, k_mb_s, v_mb_s, h_re_s, h_im_s)
```

```python
import functools
import math

import jax
import jax.numpy as jnp
from jax import lax
from jax.experimental import pallas as pl
from jax.experimental.pallas import tpu as pltpu

F32 = jnp.float32
BF16 = jnp.bfloat16

HEAD_DIM = 64
SSM_GROUP_CH = 16
MOBA_BLOCK = 256
MOBA_TOPK = 3
LN_EPS = 1e-5
LANES = 128
SUBLANES = 8
HEADS_PER_VREG = LANES // HEAD_DIM
SSM_CHUNK = 16
SSM_ROWS = 4096
NEG_BIG = -1e30
F32_EXP_ZERO = -104.0
VMEM_LIMIT = 56 * 1024 * 1024

_HI = lax.Precision.HIGHEST


def _cparams(sem):
    return pltpu.CompilerParams(dimension_semantics=sem, vmem_limit_bytes=VMEM_LIMIT)


def _resident(shape):
    return pl.BlockSpec(shape, lambda *_: (0,) * len(shape), pipeline_mode=pl.Buffered(1))


def _dot(a, b):
    return jnp.dot(a, b, preferred_element_type=F32)


def _dot_nt(a, b):
    return lax.dot_general(a, b, (((1,), (1,)), ((), ())), preferred_element_type=F32)


def _layer_norm(y, g, b):
    mu = jnp.mean(y, axis=-1, keepdims=True)
    d = y - mu
    var = jnp.mean(d * d, axis=-1, keepdims=True)
    return d * lax.rsqrt(var + LN_EPS) * g + b


def _neg_softplus(z):
    return -(jnp.maximum(z, 0.0) + jnp.log(1.0 + jnp.exp(-jnp.abs(z))))


def _split_bf16(x):
    hi = x.astype(BF16)
    lo = (x - hi.astype(F32)).astype(BF16)
    return hi, lo


def _ffn_ln_kernel(x_ref, wup_ref, wdn_ref, g_ref, b_ref, o_ref, *, alpha, d_ff, n_chunk):
    x = x_ref[...]
    xb = x.astype(BF16)
    fc = d_ff // n_chunk
    acc = jnp.zeros_like(x)
    for c in range(n_chunk):
        a = c * fc
        gate = _dot(xb, wup_ref[:, a:a + fc])
        up = _dot(xb, wup_ref[:, d_ff + a:d_ff + a + fc])
        h = (gate * jax.nn.sigmoid(gate) * up).astype(BF16)
        acc = acc + _dot(h, wdn_ref[a:a + fc, :])
    o_ref[...] = _layer_norm(alpha * x + 0.5 * acc, g_ref[...], b_ref[...])


def _ffn_ln(x, w_up, w_down, g, b, alpha, tm):
    n, d = x.shape
    d_ff = w_down.shape[0]
    n_chunk = 2 if d_ff % (2 * LANES) == 0 else 1
    return pl.pallas_call(
        functools.partial(_ffn_ln_kernel, alpha=alpha, d_ff=d_ff, n_chunk=n_chunk),
        grid=(n // tm,),
        in_specs=[pl.BlockSpec((tm, d), lambda i: (i, 0)),
                  _resident((d, 2 * d_ff)), _resident((d_ff, d)),
                  _resident((1, d)), _resident((1, d))],
        out_specs=pl.BlockSpec((tm, d), lambda i: (i, 0)),
        out_shape=jax.ShapeDtypeStruct((n, d), F32),
        compiler_params=_cparams(("parallel",)),
        name="ffn_ln",
    )(x, w_up, w_down, g, b)


def _in_proj_kernel(x_ref, w_ref, b_ref, ksb_ref, vsb_ref, kmb_ref, vmb_ref,
                    sb16_ref, mb16_ref, qmb_ref, u_ref, *maybe_ksum_ref, d_sb, d_mb, d_ssm):
    r = _dot(x_ref[...].astype(BF16), w_ref[...]) + b_ref[...]
    o = 0
    q_sb = r[:, o:o + d_sb]; o += d_sb
    k_sb = r[:, o:o + d_sb]; o += d_sb
    v_sb = r[:, o:o + d_sb]; o += d_sb
    q_mb = r[:, o:o + d_mb]; o += d_mb
    k_mb = r[:, o:o + d_mb]; o += d_mb
    v_mb = r[:, o:o + d_mb]; o += d_mb
    u = r[:, o:o + d_ssm]
    scale = HEAD_DIM ** -0.5
    ksb_ref[...] = k_sb
    vsb_ref[...] = v_sb
    kmb_ref[...] = k_mb
    vmb_ref[...] = v_mb
    sb16_ref[:, 0:d_sb] = (q_sb * scale).astype(BF16)
    sb16_ref[:, d_sb:2 * d_sb] = k_sb.astype(BF16)
    sb16_ref[:, 2 * d_sb:3 * d_sb] = v_sb.astype(BF16)
    mb16_ref[:, 0:d_mb] = (q_mb * scale).astype(BF16)
    mb16_ref[:, d_mb:2 * d_mb] = k_mb.astype(BF16)
    mb16_ref[:, 2 * d_mb:3 * d_mb] = v_mb.astype(BF16)
    qmb_ref[...] = q_mb
    u_ref[...] = u
    if maybe_ksum_ref:
        ksum_ref, = maybe_ksum_ref
        tm = k_mb.shape[0]
        for j in range(tm // MOBA_BLOCK):
            ksum_ref[0, j:j + 1, :] = jnp.sum(
                k_mb[j * MOBA_BLOCK:(j + 1) * MOBA_BLOCK, :], axis=0, keepdims=True)


def _in_proj(x, w, b, d_sb, d_mb, d_ssm, tm, with_ksum):
    n, d = x.shape
    d_out = w.shape[1]
    row = lambda width: pl.BlockSpec((tm, width), lambda i: (i, 0))
    widths = (d_sb, d_sb, d_mb, d_mb, 3 * d_sb, 3 * d_mb, d_mb, d_ssm)
    dtypes = (F32, F32, F32, F32, BF16, BF16, F32, F32)
    out_specs = [row(wd) for wd in widths]
    out_shape = [jax.ShapeDtypeStruct((n, wd), dt) for wd, dt in zip(widths, dtypes)]
    if with_ksum:
        nblk = tm // MOBA_BLOCK
        out_specs.append(pl.BlockSpec((1, nblk, d_mb), lambda i: (i, 0, 0)))
        out_shape.append(jax.ShapeDtypeStruct((n // tm, nblk, d_mb), F32))
    return pl.pallas_call(
        functools.partial(_in_proj_kernel, d_sb=d_sb, d_mb=d_mb, d_ssm=d_ssm),
        grid=(n // tm,),
        in_specs=[row(d), _resident((d, d_out)), _resident((1, d_out))],
        out_specs=out_specs,
        out_shape=out_shape,
        compiler_params=_cparams(("parallel",)),
        name="in_proj",
    )(x, w, b)


def _in_proj_prompt_kernel(x_ref, w_ref, b_ref, *refs, d_sb, d_mb, d_ssm, n_alias):
    (ksbt_ref, vsbt_ref, kmbt_ref, vmbt_ref, sb16_ref, kmb16_ref, vmbt16_ref,
     qmb_ref, u_ref, ksum_ref) = refs[n_alias:]
    r = _dot(x_ref[...].astype(BF16), w_ref[...]) + b_ref[...]
    o = 0
    q_sb = r[:, o:o + d_sb]; o += d_sb
    k_sb = r[:, o:o + d_sb]; o += d_sb
    v_sb = r[:, o:o + d_sb]; o += d_sb
    q_mb = r[:, o:o + d_mb]; o += d_mb
    k_mb = r[:, o:o + d_mb]; o += d_mb
    v_mb = r[:, o:o + d_mb]; o += d_mb
    u = r[:, o:o + d_ssm]
    scale = HEAD_DIM ** -0.5
    ksbt_ref[...] = k_sb.T
    vsbt_ref[...] = v_sb.T
    kmbt_ref[...] = k_mb.T
    v_mb_t = v_mb.T
    vmbt_ref[...] = v_mb_t
    vmbt16_ref[...] = v_mb_t.astype(BF16)
    sb16_ref[:, 0:d_sb] = (q_sb * scale).astype(BF16)
    sb16_ref[:, d_sb:2 * d_sb] = k_sb.astype(BF16)
    sb16_ref[:, 2 * d_sb:3 * d_sb] = v_sb.astype(BF16)
    kmb16_ref[...] = k_mb.astype(BF16)
    qmb_ref[...] = q_mb
    u_ref[...] = u
    tm = k_mb.shape[0]
    for j in range(tm // MOBA_BLOCK):
        ksum_ref[0, j:j + 1, :] = jnp.sum(
            k_mb[j * MOBA_BLOCK:(j + 1) * MOBA_BLOCK, :], axis=0, keepdims=True)


def _in_proj_prompt(x, w, b, layer, depth, bsz, t, d_sb, d_mb, d_ssm, tm, prev):
    n, d = x.shape
    d_out = w.shape[1]
    tpb = t // tm
    row = lambda width: pl.BlockSpec((tm, width), lambda i: (i, 0))
    kt = lambda width: pl.BlockSpec((None, None, width, tm), lambda i: (layer, i // tpb, 0, i % tpb))
    kt_shape = lambda width: jax.ShapeDtypeStruct((depth, bsz, width, t), F32)
    nblk = tm // MOBA_BLOCK
    out_specs = [kt(d_sb), kt(d_sb), kt(d_mb), kt(d_mb), row(3 * d_sb), row(d_mb),
                 pl.BlockSpec((None, d_mb, tm), lambda i: (i // tpb, 0, i % tpb)),
                 row(d_mb), row(d_ssm), pl.BlockSpec((1, nblk, d_mb), lambda i: (i, 0, 0))]
    out_shape = [kt_shape(d_sb), kt_shape(d_sb), kt_shape(d_mb), kt_shape(d_mb),
                 jax.ShapeDtypeStruct((n, 3 * d_sb), BF16), jax.ShapeDtypeStruct((n, d_mb), BF16),
                 jax.ShapeDtypeStruct((bsz, d_mb, t), BF16),
                 jax.ShapeDtypeStruct((n, d_mb), F32), jax.ShapeDtypeStruct((n, d_ssm), F32),
                 jax.ShapeDtypeStruct((n // tm, nblk, d_mb), F32)]
    in_specs = [row(d), _resident((d, d_out)), _resident((1, d_out))]
    args = [x, w, b]
    aliases = {}
    if prev is not None:
        in_specs += [pl.BlockSpec(memory_space=pl.ANY)] * len(prev)
        aliases = {len(args) + k: k for k in range(len(prev))}
        args += list(prev)
    return pl.pallas_call(
        functools.partial(_in_proj_prompt_kernel, d_sb=d_sb, d_mb=d_mb, d_ssm=d_ssm,
                          n_alias=len(aliases)),
        grid=(n // tm,),
        in_specs=in_specs,
        out_specs=out_specs,
        out_shape=out_shape,
        input_output_aliases=aliases,
        compiler_params=_cparams(("parallel",)),
        name="in_proj_prompt",
    )(*args)


def _sb_attn_kernel(q_ref, k_ref, v_ref, o_ref, *, tile):
    qi = pl.program_id(2)
    q = q_ref[...]
    lane_head = lax.broadcasted_iota(jnp.int32, (tile, LANES), 1) // HEAD_DIM
    row = lax.broadcasted_iota(jnp.int32, (tile, tile), 0)
    col = lax.broadcasted_iota(jnp.int32, (tile, tile), 1)
    tri = (row >= col).astype(BF16)

    def head_out(h):
        qh = jnp.where(lane_head == h, q, jnp.zeros_like(q))

        def cond(st):
            kt, carry, _ = st
            return jnp.logical_and(kt >= 0, jnp.max(carry) > F32_EXP_ZERO)

        def body(st):
            kt, carry, acc = st
            start = pl.multiple_of(kt * tile, tile)
            k = k_ref[pl.ds(start, tile), :]
            v = v_ref[pl.ds(start, tile), :]
            z = _dot_nt(qh, k)
            valid = jnp.logical_or(kt < qi, col < row)
            lk = jnp.where(valid, _neg_softplus(z), 0.0)
            hi, lo = _split_bf16(lk)
            rcs = _dot(hi, tri) + _dot(lo, tri)
            w = jnp.where(valid, jnp.exp(z + carry + rcs), 0.0)
            acc = acc + _dot(w.astype(BF16), v)
            return kt - 1, carry + rcs[:, 0:1], acc

        init = (qi, jnp.zeros((tile, 1), F32), jnp.zeros((tile, LANES), F32))
        return lax.while_loop(cond, body, init)[2]

    out = jnp.where(lane_head == 0, head_out(0), head_out(1))
    o_ref[...] = out.astype(o_ref.dtype)


def _sb_attn(sb16, bsz, t, n_heads):
    tile = min(256, t)
    d_sb = n_heads * HEAD_DIM
    nlb = d_sb // LANES
    x = sb16.reshape(bsz, t, 3 * d_sb)
    return pl.pallas_call(
        functools.partial(_sb_attn_kernel, tile=tile),
        grid=(bsz, nlb, t // tile),
        in_specs=[pl.BlockSpec((None, tile, LANES), lambda b, p, i: (b, i, p)),
                  pl.BlockSpec((None, t, LANES), lambda b, p, i: (b, 0, nlb + p)),
                  pl.BlockSpec((None, t, LANES), lambda b, p, i: (b, 0, 2 * nlb + p))],
        out_specs=pl.BlockSpec((None, tile, LANES), lambda b, p, i: (b, i, p)),
        out_shape=jax.ShapeDtypeStruct((bsz, t, d_sb), BF16),
        compiler_params=_cparams(("parallel", "parallel", "arbitrary")),
        name="sb_attn",
    )(x, x, x)


def _moba_select_bias(score, blk, n_blocks):
    sel = jnp.zeros(score.shape, jnp.bool_)
    s = score
    for _ in range(min(MOBA_TOPK, n_blocks)):
        m = jnp.max(s, axis=0, keepdims=True)
        idx = jnp.min(jnp.where(s == m, blk, n_blocks), axis=0, keepdims=True)
        pick = blk == idx
        sel = jnp.logical_or(sel, jnp.logical_and(pick, jnp.abs(m) < jnp.inf))
        s = jnp.where(pick, -jnp.inf, s)
    return jnp.where(sel, 0.0, NEG_BIG)


def _moba_attn_kernel(q32_ref, k_ref, vt_ref, ksum_ref, o_ref, bias_ref,
                      za_ref, zb_ref, pa_ref, pb_ref, *, n_blocks):
    qi = pl.program_id(2)
    tile = MOBA_BLOCK
    step = 2 * tile
    qt = q32_ref[...].T
    row_head = lax.broadcasted_iota(jnp.int32, (LANES, tile), 0) // HEAD_DIM
    blk = lax.broadcasted_iota(jnp.int32, (n_blocks, tile), 0)
    ksum = ksum_ref[...]
    scale = HEAD_DIM ** -0.5

    qts = []
    for h in range(HEADS_PER_VREG):
        qh = jnp.where(row_head == h, qt, 0.0)
        score = jnp.dot(ksum, qh, precision=_HI, preferred_element_type=F32) * (1.0 / MOBA_BLOCK)
        score = jnp.where(blk < qi, score, -jnp.inf)
        bias = _moba_select_bias(score, blk, n_blocks)
        for jp in range(n_blocks // 2):
            bias_ref[h, jp, 0:2, :] = bias[2 * jp:2 * jp + 2]
        qts.append((qh * scale).astype(BF16))

    key = lax.broadcasted_iota(jnp.int32, (tile, tile), 0)
    qry = lax.broadcasted_iota(jnp.int32, (tile, tile), 1)
    start = pl.multiple_of(qi * tile, tile)
    k_own = k_ref[pl.ds(start, tile), :]
    ms, ls, accs = [], [], []
    for h in range(HEADS_PER_VREG):
        zt = jnp.where(key <= qry, _dot(k_own, qts[h]), NEG_BIG)
        m = jnp.max(zt, axis=0, keepdims=True)
        p = jnp.exp(zt - m)
        ms.append(m)
        ls.append(jnp.sum(p, axis=0, keepdims=True))
        accs.append(_dot(vt_ref[h * HEAD_DIM:(h + 1) * HEAD_DIM, pl.ds(start, tile)], p.astype(BF16)))

    last_pair = n_blocks // 2 - 1
    heads = range(HEADS_PER_VREG)

    def put_logits(z_ref, j):
        start = pl.multiple_of(jnp.minimum(j, last_pair) * step, step)
        k2 = k_ref[pl.ds(start, step), :]
        for h in heads:
            z_ref[h] = _dot(k2, qts[h])

    def weighted_values(p_ref, j):
        start = pl.multiple_of(jnp.clip(j, 0, last_pair) * step, step)
        return [_dot(vt_ref[h * HEAD_DIM:(h + 1) * HEAD_DIM, pl.ds(start, step)], p_ref[h])
                for h in heads]

    def trip(j, st, z_cur, z_next, p_cur, p_prev):
        ms, ls, accs, corrs = st
        put_logits(z_next, j + 1)
        pvs = weighted_values(p_prev, j - 1)
        out_m, out_l, out_acc, out_corr = [], [], [], []
        for h in heads:
            bias = bias_ref[h, jnp.minimum(j, last_pair), 0:2, :]
            zt = jnp.concatenate([z_cur[h, :tile] + bias[0:1], z_cur[h, tile:] + bias[1:2]], axis=0)
            m_new = jnp.maximum(ms[h], jnp.max(zt, axis=0, keepdims=True))
            p = jnp.exp(zt - m_new)
            p_cur[h] = p.astype(BF16)
            corr = jnp.exp(ms[h] - m_new)
            out_m.append(m_new)
            out_l.append(ls[h] * corr + jnp.sum(p, axis=0, keepdims=True))
            out_acc.append(accs[h] * corrs[h] + pvs[h])
            out_corr.append(corr)
        return out_m, out_l, out_acc, out_corr

    def two_trips(i, st):
        st = trip(2 * i, st, za_ref, zb_ref, pa_ref, pb_ref)
        return trip(2 * i + 1, st, zb_ref, za_ref, pb_ref, pa_ref)

    put_logits(za_ref, 0)
    pb_ref[...] = jnp.zeros_like(pb_ref)
    ones = [jnp.ones((1, tile), F32) for _ in heads]
    n_trips = (qi + 1) // 2
    ms, ls, accs, corrs = lax.fori_loop(0, (n_trips + 1) // 2, two_trips, (ms, ls, accs, ones))
    pvs = weighted_values(pb_ref, 2 * ((n_trips + 1) // 2) - 1)
    out_t = jnp.concatenate([(accs[h] * corrs[h] + pvs[h]) / ls[h] for h in heads], axis=0)
    o_ref[...] = out_t.T.astype(o_ref.dtype)


def _moba_attn(k16, vt16, q_mb32, ksum, bsz, t, n_heads):
    tile = MOBA_BLOCK
    d_mb = n_heads * HEAD_DIM
    nlb = d_mb // LANES
    n_blocks = t // tile
    assert n_blocks % 2 == 0
    x = k16.reshape(bsz, t, d_mb)
    vt = vt16
    q32 = q_mb32.reshape(bsz, t, d_mb)
    return pl.pallas_call(
        functools.partial(_moba_attn_kernel, n_blocks=n_blocks),
        grid=(bsz, nlb, n_blocks),
        in_specs=[pl.BlockSpec((None, tile, LANES), lambda b, p, i: (b, i, p)),
                  pl.BlockSpec((None, t, LANES), lambda b, p, i: (b, 0, p)),
                  pl.BlockSpec((None, LANES, t), lambda b, p, i: (b, p, 0)),
                  pl.BlockSpec((None, n_blocks, LANES), lambda b, p, i: (b, 0, p))],
        out_specs=pl.BlockSpec((None, tile, LANES), lambda b, p, i: (b, i, p)),
        out_shape=jax.ShapeDtypeStruct((bsz, t, d_mb), BF16),
        scratch_shapes=[pltpu.VMEM((HEADS_PER_VREG, n_blocks // 2, SUBLANES, tile), F32)]
        + [pltpu.VMEM((HEADS_PER_VREG, 2 * tile, tile), F32)] * 2
        + [pltpu.VMEM((HEADS_PER_VREG, 2 * tile, tile), BF16)] * 2,
        compiler_params=_cparams(("parallel", "parallel", "arbitrary")),
        name="moba_attn",
    )(q32, x, vt, ksum)


def _ssm_tables(a_re, a_im, log_dt, b_re, b_im, c_re, c_im, ssm_d):
    ng, ns = a_re.shape
    nc = SSM_GROUP_CH
    L = SSM_CHUNK
    a_re, a_im = a_re.astype(F32), a_im.astype(F32)
    dt = jnp.exp(log_dt.astype(F32))[:, None]
    tau = jnp.arange(L + 1, dtype=F32)[:, None, None]
    mag = jnp.exp(tau * (a_re * dt))
    ang = tau * (a_im * dt)
    pw_re, pw_im = mag * jnp.cos(ang), mag * jnp.sin(ang)
    lam_re, lam_im = pw_re[1], pw_im[1]
    den = a_re * a_re + a_im * a_im
    num_re, num_im = lam_re - 1.0, lam_im
    coef_re = (num_re * a_re + num_im * a_im) / den
    coef_im = (num_im * a_re - num_re * a_im) / den
    b_re, b_im = b_re.astype(F32), b_im.astype(F32)
    bb_re = coef_re[..., None] * b_re - coef_im[..., None] * b_im
    bb_im = coef_re[..., None] * b_im + coef_im[..., None] * b_re
    c_re, c_im = c_re.astype(F32), c_im.astype(F32)
    x_re = pw_re[..., None] * bb_re - pw_im[..., None] * bb_im
    x_im = pw_re[..., None] * bb_im + pw_im[..., None] * bb_re
    kern = (jnp.einsum('gcp,tgpd->tgcd', c_re, x_re, precision=_HI)
            - jnp.einsum('gcp,tgpd->tgcd', c_im, x_im, precision=_HI))
    s_idx = jnp.arange(L)[:, None]
    t_idx = jnp.arange(L)[None, :]
    delta = t_idx - s_idx
    toep = jnp.where((delta >= 0)[:, :, None, None, None], kern[jnp.clip(delta, 0, L)], 0.0)
    m_intra = jnp.transpose(toep, (2, 0, 4, 1, 3)).reshape(ng, L * nc, L * nc)
    ws_re = jnp.transpose(x_re[L - 1 - jnp.arange(L)], (1, 0, 3, 2)).reshape(ng, L * nc, ns)
    ws_im = jnp.transpose(x_im[L - 1 - jnp.arange(L)], (1, 0, 3, 2)).reshape(ng, L * nc, ns)
    w_state = jnp.concatenate([ws_re, ws_im, ws_im, ws_re], axis=-1)
    cl_re = c_re[None] * jnp.transpose(pw_re[1:], (0, 1, 2))[:, :, None, :] \
        - c_im[None] * pw_im[1:][:, :, None, :]
    cl_im = c_re[None] * pw_im[1:][:, :, None, :] + c_im[None] * pw_re[1:][:, :, None, :]
    wh_re = jnp.transpose(cl_re, (1, 3, 0, 2)).reshape(ng, ns, L * nc)
    wh_im = jnp.transpose(cl_im, (1, 3, 0, 2)).reshape(ng, ns, L * nc)
    w_carry = jnp.concatenate([wh_re, -wh_im], axis=1)
    lr, li = pw_re[L], pw_im[L]
    dec_a = jnp.concatenate([lr, lr], axis=-1).reshape(1, 2 * ng * ns)
    dec_p = jnp.concatenate([-li, li], axis=-1).reshape(1, 2 * ng * ns)
    d_tile = jnp.tile(ssm_d.astype(F32).reshape(ng, 1, nc), (1, 1, L))
    eye = jnp.eye(ng, dtype=F32)
    blockdiag = lambda w: jnp.einsum('gab,gh->gahb', w, eye).reshape(ng * w.shape[1], ng * w.shape[2])
    k_lag = jnp.einsum('tgcd,gh->tgdhc', kern[:L], eye).reshape(L, ng * nc, ng * nc)
    return dict(m_intra=m_intra.astype(BF16), w_state=w_state.astype(BF16),
                w_carry=w_carry.astype(BF16), dec_a=dec_a, dec_p=dec_p, d_tile=d_tile,
                lam_re=lam_re, lam_im=lam_im, bb_re=bb_re, bb_im=bb_im,
                k_lag=k_lag.astype(BF16),
                wb_re=blockdiag(jnp.swapaxes(bb_re, 1, 2)),
                wb_im=blockdiag(jnp.swapaxes(bb_im, 1, 2)),
                wc_re=blockdiag(jnp.swapaxes(c_re, 1, 2)),
                wc_im=blockdiag(jnp.swapaxes(c_im, 1, 2)),
                pw_re=pw_re.reshape(L + 1, ng * ns), pw_im=pw_im.reshape(L + 1, ng * ns))


def _ssm_prompt_kernel(u_ref, mi_ref, ws_ref, wc_ref, da_ref, dp_ref, dt_ref,
                       y_ref, hfin_ref, sp_ref, sq_ref, hprev_ref, p_ref, q_ref,
                       *, n_groups, n_state, bp, n_chunks):
    step = pl.program_id(0)
    sw = 2 * n_state

    @pl.when(step == 0)
    def _():
        p_ref[...] = jnp.zeros_like(p_ref)
        q_ref[...] = jnp.zeros_like(q_ref)

    for g in range(n_groups):
        ug = u_ref[g]
        ugb = ug.astype(BF16)
        y_ref[g] = _dot(ugb, mi_ref[g]) + ug * dt_ref[g]
        s4 = _dot(ugb, ws_ref[g])
        sp_ref[:, g * sw:(g + 1) * sw] = s4[:, :sw]
        sq_ref[:, g * sw:(g + 1) * sw] = s4[:, sw:]

    dec_a = da_ref[...]
    dec_p = dp_ref[...]

    def scan(c, st):
        p, q = st
        r0 = pl.multiple_of(c * bp, bp)
        hprev_ref[pl.ds(r0, bp), :] = p
        p_new = dec_a * p + dec_p * q + sp_ref[pl.ds(r0, bp), :]
        q_new = dec_a * q - dec_p * p + sq_ref[pl.ds(r0, bp), :]
        return p_new, q_new

    p, q = lax.fori_loop(0, n_chunks, scan, (p_ref[...], q_ref[...]))
    p_ref[...] = p
    q_ref[...] = q
    hfin_ref[...] = p

    for g in range(n_groups):
        hg = hprev_ref[:, g * sw:(g + 1) * sw].astype(BF16)
        y_ref[g] = y_ref[g] + _dot(hg, wc_ref[g])


def _ssm_prompt(u, tabs, bsz, t, n_groups, n_state):
    L, nc = SSM_CHUNK, SSM_GROUP_CH
    bp = -(-bsz // SUBLANES) * SUBLANES
    n_chunk_total = t // L
    n_chunks = min(32, n_chunk_total)
    rows = n_chunks * bp
    width = L * nc
    ul = jnp.transpose(u.reshape(bsz, n_chunk_total, L, n_groups, nc), (3, 1, 0, 2, 4))
    ul = jnp.pad(ul, ((0, 0), (0, 0), (0, bp - bsz), (0, 0), (0, 0)))
    ul = ul.reshape(n_groups, n_chunk_total * bp, width)
    sl = 2 * n_state * n_groups
    y, hfin = pl.pallas_call(
        functools.partial(_ssm_prompt_kernel, n_groups=n_groups, n_state=n_state, bp=bp,
                          n_chunks=n_chunks),
        grid=(n_chunk_total // n_chunks,),
        in_specs=[pl.BlockSpec((n_groups, rows, width), lambda i: (0, i, 0)),
                  _resident((n_groups, width, width)),
                  _resident((n_groups, width, 4 * n_state)),
                  _resident((n_groups, 2 * n_state, width)),
                  _resident((1, sl)), _resident((1, sl)),
                  _resident((n_groups, 1, width))],
        out_specs=[pl.BlockSpec((n_groups, rows, width), lambda i: (0, i, 0)),
                   pl.BlockSpec((bp, sl), lambda i: (0, 0))],
        out_shape=[jax.ShapeDtypeStruct((n_groups, n_chunk_total * bp, width), F32),
                   jax.ShapeDtypeStruct((bp, sl), F32)],
        scratch_shapes=[pltpu.VMEM((rows, sl), F32), pltpu.VMEM((rows, sl), F32),
                        pltpu.VMEM((rows, sl), F32),
                        pltpu.VMEM((bp, sl), F32), pltpu.VMEM((bp, sl), F32)],
        compiler_params=_cparams(("arbitrary",)),
        name="ssm_prompt",
    )(ul, tabs['m_intra'], tabs['w_state'], tabs['w_carry'], tabs['dec_a'], tabs['dec_p'],
      tabs['d_tile'])
    y = y.reshape(n_groups, n_chunk_total, bp, L, nc)[:, :, :bsz]
    y = jnp.transpose(y, (2, 1, 3, 0, 4)).reshape(bsz * t, n_groups * nc)
    h = hfin[:bsz].reshape(bsz, n_groups, 2, n_state)
    return y, h[:, :, 0], h[:, :, 1]


def _ssm_rows_kernel(u_ref, klag_ref, wbr_ref, wbi_ref, wcr_ref, wci_ref, pwr_ref, pwi_ref, d_ref,
                     y_ref, hre_ref, him_ref, sre_ref, sim_ref, pre_ref, pim_ref, cre_ref, cim_ref,
                     *half_refs, n_chunks):
    L = SSM_CHUNK
    i = pl.program_id(1)

    @pl.when(i == 0)
    def _():
        cre_ref[...] = jnp.zeros_like(cre_ref)
        cim_ref[...] = jnp.zeros_like(cim_ref)

    n_half = len(half_refs) // 2
    u_half, y_half = half_refs[:n_half], half_refs[n_half:]
    for k in range(n_half):
        u_half[k][...] = u_ref[:, k * LANES:(k + 1) * LANES]
    us = [jnp.concatenate([u_half[k][pl.ds(s, n_chunks, stride=L), :] for k in range(n_half)],
                          axis=1) for s in range(L)]
    ub = [x.astype(BF16) for x in us]

    s_re = s_im = None
    for s in range(L):
        br, bi = _dot(ub[s], wbr_ref[...]), _dot(ub[s], wbi_ref[...])
        lr, li = pwr_ref[L - 1 - s:L - s, :], pwi_ref[L - 1 - s:L - s, :]
        tr, ti = lr * br - li * bi, lr * bi + li * br
        s_re, s_im = (tr, ti) if s_re is None else (s_re + tr, s_im + ti)
    sre_ref[...] = s_re
    sim_ref[...] = s_im

    lr, li = pwr_ref[L:L + 1, :], pwi_ref[L:L + 1, :]

    def chunk(c, st):
        hr, hi = st
        pre_ref[pl.ds(c, 1), :] = hr
        pim_ref[pl.ds(c, 1), :] = hi
        return (lr * hr - li * hi + sre_ref[pl.ds(c, 1), :],
                lr * hi + li * hr + sim_ref[pl.ds(c, 1), :])

    hr, hi = lax.fori_loop(0, n_chunks, chunk, (cre_ref[...], cim_ref[...]))
    cre_ref[...] = hr
    cim_ref[...] = hi
    hre_ref[...] = hr
    him_ref[...] = hi

    lr, li = pwr_ref[1:2, :], pwi_ref[1:2, :]
    g_re, g_im = pre_ref[...], pim_ref[...]
    for t in range(L):
        g_re, g_im = lr * g_re - li * g_im, lr * g_im + li * g_re
        y = (_dot(g_re.astype(BF16), wcr_ref[...]) - _dot(g_im.astype(BF16), wci_ref[...])
             + d_ref[...] * us[t])
        for s in range(t + 1):
            y = y + _dot(ub[s], klag_ref[t - s])
        for k in range(n_half):
            y_half[k][pl.ds(t, n_chunks, stride=L), :] = y[:, k * LANES:(k + 1) * LANES]
    for k in range(n_half):
        y_ref[:, k * LANES:(k + 1) * LANES] = y_half[k][...]


def _ssm_prompt(u, tabs, ssm_d, bsz, t, n_groups, n_state):
    L = SSM_CHUNK
    n, dc = u.shape
    ds = n_groups * n_state
    rows = min(t, SSM_ROWS)
    assert t % rows == 0 and rows % (L * SUBLANES) == 0
    n_chunks = rows // L
    tiles = t // rows
    bf = lambda a: a.astype(BF16)
    state = pl.BlockSpec((None, 1, ds), lambda b, i: (b, 0, 0))
    y, h_re, h_im = pl.pallas_call(
        functools.partial(_ssm_rows_kernel, n_chunks=n_chunks),
        grid=(bsz, tiles),
        in_specs=[pl.BlockSpec((rows, dc), lambda b, i: (b * tiles + i, 0)),
                  _resident((L, dc, dc)), _resident((dc, ds)), _resident((dc, ds)),
                  _resident((ds, dc)), _resident((ds, dc)),
                  _resident((L + 1, ds)), _resident((L + 1, ds)), _resident((1, dc))],
        out_specs=[pl.BlockSpec((rows, dc), lambda b, i: (b * tiles + i, 0)), state, state],
        out_shape=[jax.ShapeDtypeStruct((n, dc), F32),
                   jax.ShapeDtypeStruct((bsz, 1, ds), F32), jax.ShapeDtypeStruct((bsz, 1, ds), F32)],
        scratch_shapes=[pltpu.VMEM((n_chunks, ds), F32)] * 4 + [pltpu.VMEM((1, ds), F32)] * 2
        + [pltpu.VMEM((rows, LANES), F32)] * (2 * (dc // LANES)),
        compiler_params=_cparams(("parallel", "arbitrary")),
        name="ssm_rows",
    )(u, tabs['k_lag'], bf(tabs['wb_re']), bf(tabs['wb_im']), bf(tabs['wc_re']), bf(tabs['wc_im']),
      tabs['pw_re'], tabs['pw_im'], ssm_d.astype(F32).reshape(1, dc))
    return y, h_re.reshape(bsz, n_groups, n_state), h_im.reshape(bsz, n_groups, n_state)


def _ssm_step_kernel(u_ref, hr_ref, hi_ref, wbr_ref, wbi_ref, lr_ref, li_ref,
                     wcr_ref, wci_ref, d_ref, y_ref, hro_ref, hio_ref):
    u = u_ref[...]
    dot = lambda a, b: jnp.dot(a, b, precision=_HI, preferred_element_type=F32)
    h0r, h0i = hr_ref[...], hi_ref[...]
    lr, li = lr_ref[...], li_ref[...]
    hr = dot(u, wbr_ref[...]) + (lr * h0r - li * h0i)
    hi = dot(u, wbi_ref[...]) + (lr * h0i + li * h0r)
    hro_ref[...] = hr
    hio_ref[...] = hi
    y_ref[...] = dot(hr, wcr_ref[...]) - dot(hi, wci_ref[...]) + d_ref[...] * u


def _ssm_step(u, h0_re, h0_im, tabs, c_re, c_im, ssm_d):
    bsz, ng, ns = h0_re.shape
    nc = SSM_GROUP_CH
    eye = jnp.eye(ng, dtype=F32)
    blockdiag = lambda w: jnp.einsum('gab,gh->gahb', w, eye).reshape(ng * w.shape[1], ng * w.shape[2])
    wb_re = blockdiag(jnp.swapaxes(tabs['bb_re'], 1, 2))
    wb_im = blockdiag(jnp.swapaxes(tabs['bb_im'], 1, 2))
    wc_re = blockdiag(jnp.swapaxes(c_re.astype(F32), 1, 2))
    wc_im = blockdiag(jnp.swapaxes(c_im.astype(F32), 1, 2))
    flat = lambda a: a.astype(F32).reshape(1, ng * ns)
    y, hr, hi = pl.pallas_call(
        _ssm_step_kernel,
        out_shape=[jax.ShapeDtypeStruct((bsz, ng * nc), F32),
                   jax.ShapeDtypeStruct((bsz, ng * ns), F32),
                   jax.ShapeDtypeStruct((bsz, ng * ns), F32)],
        compiler_params=pltpu.CompilerParams(vmem_limit_bytes=VMEM_LIMIT),
        name="ssm_step",
    )(u, h0_re.astype(F32).reshape(bsz, ng * ns), h0_im.astype(F32).reshape(bsz, ng * ns),
      wb_re, wb_im, flat(tabs['lam_re']), flat(tabs['lam_im']), wc_re, wc_im,
      ssm_d.astype(F32).reshape(1, ng * nc))
    return y, hr.reshape(bsz, ng, ns), hi.reshape(bsz, ng, ns)


def _merge_kernel(x_ref, osb_ref, omb_ref, yssm_ref, wg_ref, bg_ref, wglu_ref,
                  wbsb_ref, wbmb_ref, wbssm_ref, wout_ref, g_ref, b_ref, o_ref, *, alpha):
    x = x_ref[...]
    d = x.shape[1]
    gates = jax.nn.sigmoid(_dot(x.astype(BF16), wg_ref[...]) + bg_ref[...])
    glu = _dot(yssm_ref[...].astype(BF16), wglu_ref[...])
    half = glu.shape[1] // 2
    o_ssm = glu[:, :half] * jax.nn.sigmoid(glu[:, half:])
    merged = (gates[:, 0:d] * _dot(osb_ref[...], wbsb_ref[...])
              + gates[:, d:2 * d] * _dot(omb_ref[...], wbmb_ref[...])
              + gates[:, 2 * d:3 * d] * _dot(o_ssm.astype(BF16), wbssm_ref[...]))
    mix = _dot(merged.astype(BF16), wout_ref[...])
    o_ref[...] = _layer_norm(alpha * x + mix, g_ref[...], b_ref[...])


def _merge(x, o_sb, o_mb, y_ssm, lw, alpha, tm):
    n, d = x.shape
    row = lambda a: pl.BlockSpec((tm, a.shape[1]), lambda i: (i, 0))
    acts = (x, o_sb, o_mb, y_ssm)
    weights = (lw['w_gate'], lw['b_gate'], lw['w_glu'], lw['w_br_sb'], lw['w_br_mb'],
               lw['w_br_ssm'], lw['w_out'], lw['ln2_g'], lw['ln2_b'])
    return pl.pallas_call(
        functools.partial(_merge_kernel, alpha=alpha),
        grid=(n // tm,),
        in_specs=[row(a) for a in acts] + [_resident(w.shape) for w in weights],
        out_specs=pl.BlockSpec((tm, d), lambda i: (i, 0)),
        out_shape=jax.ShapeDtypeStruct((n, d), F32),
        compiler_params=_cparams(("parallel",)),
        name="merge",
    )(*acts, *weights)


def _pages_keys_minor(cache):
    return jnp.transpose(cache, (0, 1, 3, 4, 2))


def _sb_decode_kernel(pt_ref, q_ref, ck_ref, cv_ref, o_ref, kbuf, vbuf, sem,
                      *, n_heads, n_pages, layer):
    b = pl.program_id(0)
    page = kbuf.shape[-1]
    d = n_heads * HEAD_DIM
    assert n_heads <= SUBLANES

    def page_copies(j, slot):
        pg = pt_ref[b, n_pages - 1 - j]
        return (pltpu.make_async_copy(ck_ref.at[layer, pg], kbuf.at[slot], sem.at[slot, 0]),
                pltpu.make_async_copy(cv_ref.at[layer, pg], vbuf.at[slot], sem.at[slot, 1]))

    def start_page(j, slot):
        for cp in page_copies(j, slot):
            cp.start()

    def wait_page(j, slot):
        for cp in page_copies(j, slot):
            cp.wait()

    q = q_ref[...] * (HEAD_DIM ** -0.5)
    row = lax.broadcasted_iota(jnp.int32, (page, page), 0)
    col = lax.broadcasted_iota(jnp.int32, (page, page), 1)
    tri = (row >= col).astype(BF16)
    head_rows = lax.broadcasted_iota(jnp.int32, (SUBLANES, 1), 0) < n_heads
    zero_rows = jnp.zeros((SUBLANES - n_heads, page), F32)

    def alive(carry):
        return jnp.max(jnp.where(head_rows, carry, -jnp.inf)) > F32_EXP_ZERO

    def cond(st):
        j, carry, _ = st
        return jnp.logical_and(j < n_pages, alive(carry))

    def body(st):
        j, carry, acc = st
        slot = j % 2

        @pl.when(j + 1 < n_pages)
        def _():
            start_page(j + 1, 1 - slot)

        wait_page(j, slot)
        z = jnp.concatenate(
            [jnp.sum(kbuf[slot, h] * q[h * HEAD_DIM:(h + 1) * HEAD_DIM], axis=0, keepdims=True)
             for h in range(n_heads)] + [zero_rows], axis=0)
        lk = _neg_softplus(z)
        hi, lo = _split_bf16(lk)
        rcs = _dot(hi, tri) + _dot(lo, tri)
        w = jnp.exp(z + carry + rcs)
        acc = acc + jnp.concatenate(
            [jnp.sum(vbuf[slot, h] * w[h:h + 1, :], axis=-1, keepdims=True)
             for h in range(n_heads)], axis=0)
        return j + 1, carry + rcs[:, 0:1], acc

    start_page(0, 0)
    j, _, acc = lax.while_loop(
        cond, body, (jnp.int32(0), jnp.zeros((SUBLANES, 1), F32), jnp.zeros((d, 1), F32)))

    @pl.when(j < n_pages)
    def _():
        wait_page(j, j % 2)

    o_ref[...] = acc


def _sb_decode(q, cache_k, cache_v, layer, page_table, n_heads):
    bsz, n_pages = page_table.shape
    page = cache_k.shape[2]
    d = n_heads * HEAD_DIM
    ck, cv = _pages_keys_minor(cache_k), _pages_keys_minor(cache_v)
    col = pl.BlockSpec((None, d, 1), lambda b, pt: (b, 0, 0))
    grid_spec = pltpu.PrefetchScalarGridSpec(
        num_scalar_prefetch=1,
        grid=(bsz,),
        in_specs=[col, pl.BlockSpec(memory_space=pl.ANY), pl.BlockSpec(memory_space=pl.ANY)],
        out_specs=col,
        scratch_shapes=[pltpu.VMEM((2, n_heads, HEAD_DIM, page), F32),
                        pltpu.VMEM((2, n_heads, HEAD_DIM, page), F32),
                        pltpu.SemaphoreType.DMA((2, 2))],
    )
    out = pl.pallas_call(
        functools.partial(_sb_decode_kernel, n_heads=n_heads, n_pages=n_pages, layer=layer),
        grid_spec=grid_spec,
        out_shape=jax.ShapeDtypeStruct((bsz, d, 1), F32),
        compiler_params=_cparams(("arbitrary",)),
        name="sb_decode",
    )(page_table, q.reshape(bsz, d, 1), ck, cv)
    return out.reshape(bsz, d).astype(BF16)


def _moba_ksum_kernel(pt_ref, *refs, pages_per_step, pages_per_block):
    k_refs = refs[:pages_per_step]
    o_ref = refs[pages_per_step]
    j = pl.program_id(1)
    bps = pages_per_step // pages_per_block

    @pl.when(j == 0)
    def _():
        o_ref[...] = jnp.zeros_like(o_ref)

    lane = lax.broadcasted_iota(jnp.int32, o_ref.shape, 1)
    out = o_ref[...]
    for m in range(bps):
        s = k_refs[m * pages_per_block][...]
        for r in range(1, pages_per_block):
            s = s + k_refs[m * pages_per_block + r][...]
        col = jnp.sum(s.reshape(-1, s.shape[-1]), axis=-1, keepdims=True)
        out = jnp.where(lane == j * bps + m, col, out)
    o_ref[...] = out


def _moba_ksum(cache_k, layer, page_table, n_heads):
    bsz, n_pages = page_table.shape
    page = cache_k.shape[2]
    d = n_heads * HEAD_DIM
    ppb = MOBA_BLOCK // page
    n_blocks = n_pages // ppb
    pps = math.gcd(n_blocks, 8) * ppb
    ck = _pages_keys_minor(cache_k)

    def page_spec(i):
        return pl.BlockSpec((None, None, n_heads, HEAD_DIM, page),
                            lambda b, j, pt: (layer, pt[b, j * pps + i], 0, 0, 0))

    grid_spec = pltpu.PrefetchScalarGridSpec(
        num_scalar_prefetch=1,
        grid=(bsz, n_pages // pps),
        in_specs=[page_spec(i) for i in range(pps)],
        out_specs=pl.BlockSpec((None, d, n_blocks), lambda b, j, pt: (b, 0, 0)),
    )
    return pl.pallas_call(
        functools.partial(_moba_ksum_kernel, pages_per_step=pps, pages_per_block=ppb),
        grid_spec=grid_spec,
        out_shape=jax.ShapeDtypeStruct((bsz, d, n_blocks), F32),
        compiler_params=_cparams(("parallel", "arbitrary")),
        name="moba_ksum",
    )(page_table, *([ck] * pps))


def _moba_topk_kernel(q_ref, ksum_ref, idx_ref, ok_ref, *, n_heads, n_blocks):
    prod = ksum_ref[...] * q_ref[...]
    score = jnp.sum(prod.reshape(n_heads, HEAD_DIM, n_blocks), axis=1) * (1.0 / MOBA_BLOCK)
    blk = lax.broadcasted_iota(jnp.int32, (n_heads, n_blocks), 1)
    lane = lax.broadcasted_iota(jnp.int32, (n_heads, LANES), 1)
    s = score
    idx_out = jnp.zeros((n_heads, LANES), jnp.int32)
    ok_out = jnp.zeros((n_heads, LANES), jnp.int32)
    for r in range(MOBA_TOPK):
        m = jnp.max(s, axis=-1, keepdims=True)
        idx = jnp.min(jnp.where(s == m, blk, n_blocks), axis=-1, keepdims=True)
        idx_out = jnp.where(lane == r, jnp.minimum(idx, n_blocks - 1), idx_out)
        ok_out = jnp.where(lane == r, (jnp.abs(m) < jnp.inf).astype(jnp.int32), ok_out)
        s = jnp.where(blk == idx, -jnp.inf, s)
    idx_ref[...] = idx_out
    ok_ref[...] = ok_out


def _moba_topk(q, ksum_t, n_heads):
    bsz, d, n_blocks = ksum_t.shape
    assert n_blocks >= MOBA_TOPK
    idx, ok = pl.pallas_call(
        functools.partial(_moba_topk_kernel, n_heads=n_heads, n_blocks=n_blocks),
        grid=(bsz,),
        in_specs=[pl.BlockSpec((None, d, 1), lambda b: (b, 0, 0)),
                  pl.BlockSpec((None, d, n_blocks), lambda b: (b, 0, 0))],
        out_specs=[pl.BlockSpec((None, n_heads, LANES), lambda b: (b, 0, 0))] * 2,
        out_shape=[jax.ShapeDtypeStruct((bsz, n_heads, LANES), jnp.int32)] * 2,
        compiler_params=_cparams(("parallel",)),
        name="moba_topk",
    )(q.reshape(bsz, d, 1), ksum_t)
    return idx[:, :, :MOBA_TOPK], ok[:, :, :MOBA_TOPK]


def _moba_decode_kernel(pt_ref, idx_ref, ok_ref, q_ref, kn_ref, vn_ref, *refs, n_tiles):
    k_refs = refs[:n_tiles]
    v_refs = refs[n_tiles:2 * n_tiles]
    o_ref = refs[2 * n_tiles]
    b = pl.program_id(0)
    h = pl.program_id(1)
    n_heads = pl.num_programs(1)
    ppb = n_tiles // MOBA_TOPK
    q = q_ref[...] * (HEAD_DIM ** -0.5)
    z_own = jnp.sum(q * kn_ref[...], axis=0, keepdims=True)
    zs = []
    for t in range(n_tiles):
        ok = ok_ref[(b * n_heads + h) * MOBA_TOPK + t // ppb] > 0
        z = jnp.sum(k_refs[t][...] * q, axis=0, keepdims=True)
        zs.append(jnp.where(ok, z, NEG_BIG))
    m = z_own
    for z in zs:
        m = jnp.maximum(m, jnp.max(z, axis=-1, keepdims=True))
    p_own = jnp.exp(z_own - m)
    l = p_own
    acc = p_own * vn_ref[...]
    for t in range(n_tiles):
        pw = jnp.exp(zs[t] - m)
        l = l + jnp.sum(pw, axis=-1, keepdims=True)
        acc = acc + jnp.sum(v_refs[t][...] * pw, axis=-1, keepdims=True)
    o_ref[...] = acc / l


def _moba_decode(q, k_new, v_new, idx, ok, cache_k, cache_v, layer, page_table, n_heads):
    bsz, n_pages = page_table.shape
    page = cache_k.shape[2]
    d = n_heads * HEAD_DIM
    ppb = MOBA_BLOCK // page
    n_tiles = MOBA_TOPK * ppb
    ck, cv = _pages_keys_minor(cache_k), _pages_keys_minor(cache_v)

    def tile_spec(t):
        def index_map(b, h, pt, idx, ok):
            blk = idx[(b * n_heads + h) * MOBA_TOPK + t // ppb]
            return (layer, pt[b, blk * ppb + t % ppb], h, 0, 0)
        return pl.BlockSpec((None, None, None, HEAD_DIM, page), index_map)

    tiles = [tile_spec(t) for t in range(n_tiles)]
    col = pl.BlockSpec((None, HEAD_DIM, 1), lambda b, h, pt, idx, ok: (b, h, 0))
    grid_spec = pltpu.PrefetchScalarGridSpec(
        num_scalar_prefetch=3,
        grid=(bsz, n_heads),
        in_specs=[col, col, col] + tiles * 2,
        out_specs=col,
    )
    out = pl.pallas_call(
        functools.partial(_moba_decode_kernel, n_tiles=n_tiles),
        grid_spec=grid_spec,
        out_shape=jax.ShapeDtypeStruct((bsz, d, 1), F32),
        compiler_params=_cparams(("parallel", "parallel")),
        name="moba_decode",
    )(page_table, idx.reshape(-1), ok.reshape(-1), q.reshape(bsz, d, 1), k_new.reshape(bsz, d, 1),
      v_new.reshape(bsz, d, 1), *([ck] * n_tiles), *([cv] * n_tiles))
    return out.reshape(bsz, d).astype(BF16)


def _row_tile(n):
    for tm in (512, 256, 128, 64, 32, 16, 8):
        if n % tm == 0:
            return tm
    return n


def _layer_weights(w, l, d_model, d_sb, d_mb, d_ssm):
    n_qkvu = 3 * d_sb + 3 * d_mb + d_ssm
    bf = lambda a: a.astype(BF16)
    vec = lambda a: a.astype(F32).reshape(1, -1)
    lw = dict(
        w_ffn1_up=bf(w['w_ffn1_up'][l]), w_ffn1_down=bf(w['w_ffn1_down'][l]),
        w_ffn2_up=bf(w['w_ffn2_up'][l]), w_ffn2_down=bf(w['w_ffn2_down'][l]),
        w_qkvu=bf(w['w_in'][l][:, :n_qkvu]), b_qkvu=vec(w['b_in'][l][:n_qkvu]),
        w_gate=bf(w['w_in'][l][:, n_qkvu:]), b_gate=vec(w['b_in'][l][n_qkvu:]),
        w_glu=bf(w['w_glu'][l]), w_br_sb=bf(w['w_br_sb'][l]), w_br_mb=bf(w['w_br_mb'][l]),
        w_br_ssm=bf(w['w_br_ssm'][l]), w_out=bf(w['w_out'][l]),
    )
    for name in ('ln1_g', 'ln1_b', 'ln2_g', 'ln2_b', 'ln3_g', 'ln3_b'):
        lw[name] = vec(w[name][l])
    return lw


def kernel(x_prompt, x_sample, cache_k_sb, cache_v_sb, cache_k_mb, cache_v_mb, state_ssm_re, state_ssm_im, page_table, ln1_g, ln1_b, w_ffn1_up, w_ffn1_down, w_in, b_in, ssm_a_re, ssm_a_im, ssm_log_dt, ssm_b_re, ssm_b_im, ssm_c_re, ssm_c_im, ssm_d, w_glu, w_br_sb, w_br_mb, w_br_ssm, w_out, ln2_g, ln2_b, w_ffn2_up, w_ffn2_down, ln3_g, ln3_b):
    weights = dict(ln1_g=ln1_g, ln1_b=ln1_b, w_ffn1_up=w_ffn1_up, w_ffn1_down=w_ffn1_down,
                   w_in=w_in, b_in=b_in, w_glu=w_glu, w_br_sb=w_br_sb, w_br_mb=w_br_mb,
                   w_br_ssm=w_br_ssm, w_out=w_out, ln2_g=ln2_g, ln2_b=ln2_b,
                   w_ffn2_up=w_ffn2_up, w_ffn2_down=w_ffn2_down, ln3_g=ln3_g, ln3_b=ln3_b)
    depth = w_in.shape[0]
    bsz, t, d_model = x_prompt.shape
    dbsz, dec_t, _ = x_sample.shape
    h_sb, h_mb = cache_k_sb.shape[3], cache_k_mb.shape[3]
    n_groups, n_state = ssm_a_re.shape[1:]
    d_sb, d_mb, d_ssm = h_sb * HEAD_DIM, h_mb * HEAD_DIM, n_groups * SSM_GROUP_CH
    page = cache_k_sb.shape[2]
    past_len = page_table.shape[1] * page
    assert dec_t == 1 and t % MOBA_BLOCK == 0 and t % SSM_CHUNK == 0
    assert MOBA_BLOCK % page == 0 and past_len % MOBA_BLOCK == 0
    assert d_sb % LANES == 0 and d_mb % LANES == 0
    alpha = float((2 * depth) ** 0.25)

    n_p = bsz * t
    tm_p = _row_tile(n_p)
    if tm_p % MOBA_BLOCK:
        tm_p = MOBA_BLOCK
    tm_s = _row_tile(dbsz)

    y_p = x_prompt.reshape(n_p, d_model)
    y_s = x_sample.reshape(dbsz, d_model)
    prompt_states, sample_states = [], []
    prompt_kv = None
    for l in range(depth):
        lw = _layer_weights(weights, l, d_model, d_sb, d_mb, d_ssm)
        tabs = _ssm_tables(ssm_a_re[l], ssm_a_im[l], ssm_log_dt[l], ssm_b_re[l], ssm_b_im[l],
                           ssm_c_re[l], ssm_c_im[l], ssm_d[l])

        x1 = _ffn_ln(y_p, lw['w_ffn1_up'], lw['w_ffn1_down'], lw['ln1_g'], lw['ln1_b'], alpha, tm_p)
        (*prompt_kv, sb16, kmb16, vmbt16, q_mb32, u, ksum) = _in_proj_prompt(
            x1, lw['w_qkvu'], lw['b_qkvu'], l, depth, bsz, t, d_sb, d_mb, d_ssm, tm_p, prompt_kv)
        o_sb = _sb_attn(sb16, bsz, t, h_sb).reshape(n_p, d_sb)
        o_mb = _moba_attn(kmb16, vmbt16, q_mb32, ksum.reshape(bsz, t // MOBA_BLOCK, d_mb),
                          bsz, t, h_mb).reshape(n_p, d_mb)
        y_ssm, h_re, h_im = _ssm_prompt(u, tabs, ssm_d[l], bsz, t, n_groups, n_state)
        x2 = _merge(x1, o_sb, o_mb, y_ssm, lw, alpha, tm_p)
        y_p = _ffn_ln(x2, lw['w_ffn2_up'], lw['w_ffn2_down'], lw['ln3_g'], lw['ln3_b'], alpha, tm_p)
        prompt_states.append((h_re.astype(state_ssm_re.dtype), h_im.astype(state_ssm_im.dtype)))

        s1 = _ffn_ln(y_s, lw['w_ffn1_up'], lw['w_ffn1_down'], lw['ln1_g'], lw['ln1_b'], alpha, tm_s)
        (k_sb, v_sb, k_mb, v_mb, sb16, mb16, q_mb32, u) = _in_proj(
            s1, lw['w_qkvu'], lw['b_qkvu'], d_sb, d_mb, d_ssm, tm_s, False)
        q_sb32 = sb16[:, :d_sb].astype(F32) * (HEAD_DIM ** 0.5)
        o_sb = _sb_decode(q_sb32, cache_k_sb, cache_v_sb, l, page_table, h_sb)
        ksum = _moba_ksum(cache_k_mb, l, page_table, h_mb)
        idx, ok = _moba_topk(q_mb32, ksum, h_mb)
        o_mb = _moba_decode(q_mb32, k_mb, v_mb, idx, ok, cache_k_mb, cache_v_mb, l, page_table, h_mb)
        y_ssm, h_re, h_im = _ssm_step(u, state_ssm_re[l], state_ssm_im[l], tabs,
                                      ssm_c_re[l], ssm_c_im[l], ssm_d[l])
        s2 = _merge(s1, o_sb, o_mb, y_ssm, lw, alpha, tm_s)
        y_s = _ffn_ln(s2, lw['w_ffn2_up'], lw['w_ffn2_down'], lw['ln3_g'], lw['ln3_b'], alpha, tm_s)
        sample_states.append((k_sb.reshape(dbsz, 1, h_sb, HEAD_DIM), v_sb.reshape(dbsz, 1, h_sb, HEAD_DIM),
                              k_mb.reshape(dbsz, 1, h_mb, HEAD_DIM), v_mb.reshape(dbsz, 1, h_mb, HEAD_DIM),
                              h_re.astype(state_ssm_re.dtype), h_im.astype(state_ssm_im.dtype)))

    stack = lambda states: tuple(jnp.stack(group) for group in zip(*states))
    h_re_p, h_im_p = stack(prompt_states)
    rows_major = lambda a, h: jnp.transpose(a.reshape(depth, bsz, h, HEAD_DIM, t), (0, 1, 4, 2, 3))
    k_sb_p, v_sb_p = rows_major(prompt_kv[0], h_sb), rows_major(prompt_kv[1], h_sb)
    k_mb_p, v_mb_p = rows_major(prompt_kv[2], h_mb), rows_major(prompt_kv[3], h_mb)
    k_sb_s, v_sb_s, k_mb_s, v_mb_s, h_re_s, h_im_s = stack(sample_states)
    return (y_p.reshape(bsz, t, d_model), y_s.reshape(dbsz, dec_t, d_model),
            k_sb_p, v_sb_p, k_mb_p, v_mb_p, h_re_p, h_im_p,
            k_sb_s, v_sb_s, k_mb_s, v_mb_s, h_re_s, h_im_s)
```

```python
import functools
import math

import jax
import jax.numpy as jnp
from jax import lax
from jax.experimental import pallas as pl
from jax.experimental.pallas import tpu as pltpu

F32 = jnp.float32
BF16 = jnp.bfloat16

HEAD_DIM = 64
SSM_GROUP_CH = 16
MOBA_BLOCK = 256
MOBA_TOPK = 3
LN_EPS = 1e-5
LANES = 128
SUBLANES = 8
HEADS_PER_VREG = LANES // HEAD_DIM
SSM_CHUNK = 16
SSM_ROWS = 4096
NEG_BIG = -1e30
LOG2_E = 1.4426950408889634
F32_EXP_ZERO = -104.0
VMEM_LIMIT = 56 * 1024 * 1024

_HI = lax.Precision.HIGHEST


def _cparams(sem):
    return pltpu.CompilerParams(dimension_semantics=sem, vmem_limit_bytes=VMEM_LIMIT)


def _resident(shape):
    return pl.BlockSpec(shape, lambda *_: (0,) * len(shape), pipeline_mode=pl.Buffered(1))


def _dot(a, b):
    return jnp.dot(a, b, preferred_element_type=F32)


def _dot_nt(a, b):
    return lax.dot_general(a, b, (((1,), (1,)), ((), ())), preferred_element_type=F32)


def _layer_norm(y, g, b):
    mu = jnp.mean(y, axis=-1, keepdims=True)
    d = y - mu
    var = jnp.mean(d * d, axis=-1, keepdims=True)
    return d * lax.rsqrt(var + LN_EPS) * g + b


def _neg_softplus(z):
    return -(jnp.maximum(z, 0.0) + jnp.log(1.0 + jnp.exp(-jnp.abs(z))))


def _split_bf16(x):
    hi = x.astype(BF16)
    lo = (x - hi.astype(F32)).astype(BF16)
    return hi, lo


def _ffn_ln_kernel(x_ref, wup_ref, wdn_ref, g_ref, b_ref, o_ref, *, alpha, d_ff, n_chunk):
    x = x_ref[...]
    xb = x.astype(BF16)
    fc = d_ff // n_chunk
    acc = jnp.zeros_like(x)
    for c in range(n_chunk):
        a = c * fc
        gate = _dot(xb, wup_ref[:, a:a + fc])
        up = _dot(xb, wup_ref[:, d_ff + a:d_ff + a + fc])
        h = (gate * jax.nn.sigmoid(gate) * up).astype(BF16)
        acc = acc + _dot(h, wdn_ref[a:a + fc, :])
    o_ref[...] = _layer_norm(alpha * x + 0.5 * acc, g_ref[...], b_ref[...])


def _ffn_ln(x, w_up, w_down, g, b, alpha, tm):
    n, d = x.shape
    d_ff = w_down.shape[0]
    n_chunk = 2 if d_ff % (2 * LANES) == 0 else 1
    return pl.pallas_call(
        functools.partial(_ffn_ln_kernel, alpha=alpha, d_ff=d_ff, n_chunk=n_chunk),
        grid=(n // tm,),
        in_specs=[pl.BlockSpec((tm, d), lambda i: (i, 0)),
                  _resident((d, 2 * d_ff)), _resident((d_ff, d)),
                  _resident((1, d)), _resident((1, d))],
        out_specs=pl.BlockSpec((tm, d), lambda i: (i, 0)),
        out_shape=jax.ShapeDtypeStruct((n, d), F32),
        compiler_params=_cparams(("parallel",)),
        name="ffn_ln",
    )(x, w_up, w_down, g, b)


def _in_proj_kernel(x_ref, w_ref, b_ref, ksb_ref, vsb_ref, kmb_ref, vmb_ref,
                    sb16_ref, mb16_ref, qmb_ref, u_ref, *, d_sb, d_mb, d_ssm):
    r = _dot(x_ref[...].astype(BF16), w_ref[...]) + b_ref[...]
    o = 0
    q_sb = r[:, o:o + d_sb]; o += d_sb
    k_sb = r[:, o:o + d_sb]; o += d_sb
    v_sb = r[:, o:o + d_sb]; o += d_sb
    q_mb = r[:, o:o + d_mb]; o += d_mb
    k_mb = r[:, o:o + d_mb]; o += d_mb
    v_mb = r[:, o:o + d_mb]; o += d_mb
    u = r[:, o:o + d_ssm]
    scale = HEAD_DIM ** -0.5
    ksb_ref[...] = k_sb
    vsb_ref[...] = v_sb
    kmb_ref[...] = k_mb
    vmb_ref[...] = v_mb
    sb16_ref[:, 0:d_sb] = (q_sb * scale).astype(BF16)
    sb16_ref[:, d_sb:2 * d_sb] = k_sb.astype(BF16)
    sb16_ref[:, 2 * d_sb:3 * d_sb] = v_sb.astype(BF16)
    mb16_ref[:, 0:d_mb] = (q_mb * scale).astype(BF16)
    mb16_ref[:, d_mb:2 * d_mb] = k_mb.astype(BF16)
    mb16_ref[:, 2 * d_mb:3 * d_mb] = v_mb.astype(BF16)
    qmb_ref[...] = q_mb
    u_ref[...] = u


def _in_proj(x, w, b, d_sb, d_mb, d_ssm, tm):
    n, d = x.shape
    d_out = w.shape[1]
    row = lambda width: pl.BlockSpec((tm, width), lambda i: (i, 0))
    widths = (d_sb, d_sb, d_mb, d_mb, 3 * d_sb, 3 * d_mb, d_mb, d_ssm)
    dtypes = (F32, F32, F32, F32, BF16, BF16, F32, F32)
    out_specs = [row(wd) for wd in widths]
    out_shape = [jax.ShapeDtypeStruct((n, wd), dt) for wd, dt in zip(widths, dtypes)]
    return pl.pallas_call(
        functools.partial(_in_proj_kernel, d_sb=d_sb, d_mb=d_mb, d_ssm=d_ssm),
        grid=(n // tm,),
        in_specs=[row(d), _resident((d, d_out)), _resident((1, d_out))],
        out_specs=out_specs,
        out_shape=out_shape,
        compiler_params=_cparams(("parallel",)),
        name="in_proj",
    )(x, w, b)


def _in_proj_prompt_kernel(x_ref, w_ref, b_ref, *refs, d_sb, d_mb, d_ssm, n_alias):
    (ksbt_ref, vsbt_ref, kmbt_ref, vmbt_ref, sb16_ref, kmb16_ref, vmbt16_ref,
     qmb_ref, u_ref, ksum_ref) = refs[n_alias:]
    r = _dot(x_ref[...].astype(BF16), w_ref[...]) + b_ref[...]
    o = 0
    q_sb = r[:, o:o + d_sb]; o += d_sb
    k_sb = r[:, o:o + d_sb]; o += d_sb
    v_sb = r[:, o:o + d_sb]; o += d_sb
    q_mb = r[:, o:o + d_mb]; o += d_mb
    k_mb = r[:, o:o + d_mb]; o += d_mb
    v_mb = r[:, o:o + d_mb]; o += d_mb
    u = r[:, o:o + d_ssm]
    scale = HEAD_DIM ** -0.5
    ksbt_ref[...] = k_sb.T
    vsbt_ref[...] = v_sb.T
    kmbt_ref[...] = k_mb.T
    v_mb_t = v_mb.T
    vmbt_ref[...] = v_mb_t
    vmbt16_ref[...] = v_mb_t.astype(BF16)
    sb16_ref[:, 0:d_sb] = (q_sb * scale).astype(BF16)
    sb16_ref[:, d_sb:2 * d_sb] = k_sb.astype(BF16)
    sb16_ref[:, 2 * d_sb:3 * d_sb] = v_sb.astype(BF16)
    kmb16_ref[...] = k_mb.astype(BF16)
    qmb_ref[...] = q_mb
    u_ref[...] = u
    tm = k_mb.shape[0]
    for j in range(tm // MOBA_BLOCK):
        ksum_ref[0, j:j + 1, :] = jnp.sum(
            k_mb[j * MOBA_BLOCK:(j + 1) * MOBA_BLOCK, :], axis=0, keepdims=True)


def _in_proj_prompt(x, w, b, layer, depth, bsz, t, d_sb, d_mb, d_ssm, tm, prev):
    n, d = x.shape
    d_out = w.shape[1]
    tpb = t // tm
    row = lambda width: pl.BlockSpec((tm, width), lambda i: (i, 0))
    kt = lambda width: pl.BlockSpec((None, None, width, tm), lambda i: (layer, i // tpb, 0, i % tpb))
    kt_shape = lambda width: jax.ShapeDtypeStruct((depth, bsz, width, t), F32)
    nblk = tm // MOBA_BLOCK
    out_specs = [kt(d_sb), kt(d_sb), kt(d_mb), kt(d_mb), row(3 * d_sb), row(d_mb),
                 pl.BlockSpec((None, d_mb, tm), lambda i: (i // tpb, 0, i % tpb)),
                 row(d_mb), row(d_ssm), pl.BlockSpec((1, nblk, d_mb), lambda i: (i, 0, 0))]
    out_shape = [kt_shape(d_sb), kt_shape(d_sb), kt_shape(d_mb), kt_shape(d_mb),
                 jax.ShapeDtypeStruct((n, 3 * d_sb), BF16), jax.ShapeDtypeStruct((n, d_mb), BF16),
                 jax.ShapeDtypeStruct((bsz, d_mb, t), BF16),
                 jax.ShapeDtypeStruct((n, d_mb), F32), jax.ShapeDtypeStruct((n, d_ssm), F32),
                 jax.ShapeDtypeStruct((n // tm, nblk, d_mb), F32)]
    in_specs = [row(d), _resident((d, d_out)), _resident((1, d_out))]
    args = [x, w, b]
    aliases = {}
    if prev is not None:
        in_specs += [pl.BlockSpec(memory_space=pl.ANY)] * len(prev)
        aliases = {len(args) + k: k for k in range(len(prev))}
        args += list(prev)
    return pl.pallas_call(
        functools.partial(_in_proj_prompt_kernel, d_sb=d_sb, d_mb=d_mb, d_ssm=d_ssm,
                          n_alias=len(aliases)),
        grid=(n // tm,),
        in_specs=in_specs,
        out_specs=out_specs,
        out_shape=out_shape,
        input_output_aliases=aliases,
        compiler_params=_cparams(("parallel",)),
        name="in_proj_prompt",
    )(*args)


def _sb_attn_kernel(q_ref, k_ref, v_ref, o_ref, *, tile):
    qi = pl.program_id(2)
    q = q_ref[...]
    lane_head = lax.broadcasted_iota(jnp.int32, (tile, LANES), 1) // HEAD_DIM
    row = lax.broadcasted_iota(jnp.int32, (tile, tile), 0)
    col = lax.broadcasted_iota(jnp.int32, (tile, tile), 1)
    tri = (row >= col).astype(BF16)

    heads = range(HEADS_PER_VREG)
    qhs = [jnp.where(lane_head == h, q, jnp.zeros_like(q)) for h in heads]

    def cond(st):
        kt, carries, _ = st
        top = functools.reduce(jnp.maximum, carries)
        return jnp.logical_and(kt >= 0, jnp.max(top) > F32_EXP_ZERO)

    def body(st):
        kt, carries, accs = st
        start = pl.multiple_of(kt * tile, tile)
        k = k_ref[pl.ds(start, tile), :]
        v = v_ref[pl.ds(start, tile), :]
        valid = jnp.logical_or(kt < qi, col < row)
        zs = [_dot_nt(qhs[h], k) for h in heads]
        lks = [jnp.where(valid, _neg_softplus(zs[h]), 0.0) for h in heads]
        parts = [_split_bf16(lks[h]) for h in heads]
        rcss = [_dot(parts[h][0], tri) + _dot(parts[h][1], tri) for h in heads]
        ws = [jnp.where(valid, jnp.exp(zs[h] + carries[h] + rcss[h]), 0.0) for h in heads]
        accs = [accs[h] + _dot(ws[h].astype(BF16), v) for h in heads]
        return kt - 1, [carries[h] + rcss[h][:, 0:1] for h in heads], accs

    init = (qi, [jnp.zeros((tile, 1), F32) for _ in heads],
            [jnp.zeros((tile, LANES), F32) for _ in heads])
    accs = lax.while_loop(cond, body, init)[2]
    out = jnp.where(lane_head == 0, accs[0], accs[1])
    o_ref[...] = out.astype(o_ref.dtype)


def _sb_attn(sb16, bsz, t, n_heads):
    tile = min(256, t)
    d_sb = n_heads * HEAD_DIM
    nlb = d_sb // LANES
    x = sb16.reshape(bsz, t, 3 * d_sb)
    return pl.pallas_call(
        functools.partial(_sb_attn_kernel, tile=tile),
        grid=(bsz, nlb, t // tile),
        in_specs=[pl.BlockSpec((None, tile, LANES), lambda b, p, i: (b, i, p)),
                  pl.BlockSpec((None, t, LANES), lambda b, p, i: (b, 0, nlb + p)),
                  pl.BlockSpec((None, t, LANES), lambda b, p, i: (b, 0, 2 * nlb + p))],
        out_specs=pl.BlockSpec((None, tile, LANES), lambda b, p, i: (b, i, p)),
        out_shape=jax.ShapeDtypeStruct((bsz, t, d_sb), BF16),
        compiler_params=_cparams(("parallel", "parallel", "arbitrary")),
        name="sb_attn",
    )(x, x, x)


def _moba_select_bias(score, blk, n_blocks):
    sel = jnp.zeros(score.shape, jnp.bool_)
    s = score
    for _ in range(min(MOBA_TOPK, n_blocks)):
        m = jnp.max(s, axis=0, keepdims=True)
        idx = jnp.min(jnp.where(s == m, blk, n_blocks), axis=0, keepdims=True)
        pick = blk == idx
        sel = jnp.logical_or(sel, jnp.logical_and(pick, jnp.abs(m) < jnp.inf))
        s = jnp.where(pick, -jnp.inf, s)
    return jnp.where(sel, 0.0, NEG_BIG)


def _moba_attn_kernel(q32_ref, k_ref, vt_ref, ksum_ref, o_ref, bias_ref,
                      za_ref, zb_ref, pa_ref, pb_ref, *, n_blocks):
    qi = pl.program_id(2)
    tile = MOBA_BLOCK
    step = 2 * tile
    qt = q32_ref[...].T
    row_head = lax.broadcasted_iota(jnp.int32, (LANES, tile), 0) // HEAD_DIM
    blk = lax.broadcasted_iota(jnp.int32, (n_blocks, tile), 0)
    ksum = ksum_ref[...]
    scale = HEAD_DIM ** -0.5 * LOG2_E

    heads = range(HEADS_PER_VREG)
    last_pair = n_blocks // 2 - 1
    qhs = [jnp.where(row_head == h, qt, 0.0) for h in heads]
    qts = [(qhs[h] * scale).astype(BF16) for h in heads]

    def put_logits(z_ref, j):
        start = pl.multiple_of(jnp.minimum(j, last_pair) * step, step)
        k2 = k_ref[pl.ds(start, step), :]
        for h in heads:
            z_ref[h] = _dot(k2, qts[h])

    put_logits(za_ref, 0)
    start = pl.multiple_of(qi * tile, tile)
    k_own = k_ref[pl.ds(start, tile), :]
    z_own = [_dot(k_own, qts[h]) for h in heads]

    for h in heads:
        score = jnp.dot(ksum, qhs[h], precision=_HI, preferred_element_type=F32) * (1.0 / MOBA_BLOCK)
        score = jnp.where(blk < qi, score, -jnp.inf)
        bias = _moba_select_bias(score, blk, n_blocks)
        for jp in range(n_blocks // 2):
            bias_ref[h, jp, 0:2, :] = bias[2 * jp:2 * jp + 2]

    key = lax.broadcasted_iota(jnp.int32, (tile, tile), 0)
    qry = lax.broadcasted_iota(jnp.int32, (tile, tile), 1)
    ms, ls, accs = [], [], []
    for h in heads:
        zt = jnp.where(key <= qry, z_own[h], NEG_BIG)
        m = jnp.max(zt, axis=0, keepdims=True)
        p = jnp.exp2(zt - m)
        ms.append(m)
        ls.append(jnp.sum(p, axis=0, keepdims=True))
        accs.append(_dot(vt_ref[h * HEAD_DIM:(h + 1) * HEAD_DIM, pl.ds(start, tile)], p.astype(BF16)))

    def weighted_values(p_ref, j):
        start = pl.multiple_of(jnp.clip(j, 0, last_pair) * step, step)
        return [_dot(vt_ref[h * HEAD_DIM:(h + 1) * HEAD_DIM, pl.ds(start, step)], p_ref[h])
                for h in heads]

    def trip(j, st, z_cur, z_next, p_cur, p_prev):
        ms, ls, accs, corrs = st
        put_logits(z_next, j + 1)
        pvs = weighted_values(p_prev, j - 1)
        out_m, out_l, out_acc, out_corr = [], [], [], []
        for h in heads:
            bias = bias_ref[h, jnp.minimum(j, last_pair), 0:2, :]
            halves = (z_cur[h, :tile], z_cur[h, tile:])
            tops = [jnp.max(halves[r], axis=0, keepdims=True) + bias[r:r + 1] for r in range(2)]
            m_new = jnp.maximum(ms[h], jnp.maximum(tops[0], tops[1]))
            p = jnp.concatenate([jnp.exp2(halves[r] - (m_new - bias[r:r + 1])) for r in range(2)],
                                axis=0)
            p_cur[h] = p.astype(BF16)
            corr = jnp.exp2(ms[h] - m_new)
            out_m.append(m_new)
            out_l.append(ls[h] * corr + jnp.sum(p, axis=0, keepdims=True))
            out_acc.append(accs[h] * corrs[h] + pvs[h])
            out_corr.append(corr)
        return out_m, out_l, out_acc, out_corr

    def two_trips(i, st):
        st = trip(2 * i, st, za_ref, zb_ref, pa_ref, pb_ref)
        return trip(2 * i + 1, st, zb_ref, za_ref, pb_ref, pa_ref)

    pb_ref[...] = jnp.zeros_like(pb_ref)
    ones = [jnp.ones((1, tile), F32) for _ in heads]
    n_trips = (qi + 1) // 2
    ms, ls, accs, corrs = lax.fori_loop(0, (n_trips + 1) // 2, two_trips, (ms, ls, accs, ones))
    pvs = weighted_values(pb_ref, 2 * ((n_trips + 1) // 2) - 1)
    out_t = jnp.concatenate([(accs[h] * corrs[h] + pvs[h]) / ls[h] for h in heads], axis=0)
    o_ref[...] = out_t.T.astype(o_ref.dtype)


def _moba_attn(k16, vt16, q_mb32, ksum, bsz, t, n_heads):
    tile = MOBA_BLOCK
    d_mb = n_heads * HEAD_DIM
    nlb = d_mb // LANES
    n_blocks = t // tile
    assert n_blocks % 2 == 0
    x = k16.reshape(bsz, t, d_mb)
    vt = vt16
    q32 = q_mb32.reshape(bsz, t, d_mb)
    return pl.pallas_call(
        functools.partial(_moba_attn_kernel, n_blocks=n_blocks),
        grid=(bsz, nlb, n_blocks),
        in_specs=[pl.BlockSpec((None, tile, LANES), lambda b, p, i: (b, i, p)),
                  pl.BlockSpec((None, t, LANES), lambda b, p, i: (b, 0, p)),
                  pl.BlockSpec((None, LANES, t), lambda b, p, i: (b, p, 0)),
                  pl.BlockSpec((None, n_blocks, LANES), lambda b, p, i: (b, 0, p))],
        out_specs=pl.BlockSpec((None, tile, LANES), lambda b, p, i: (b, i, p)),
        out_shape=jax.ShapeDtypeStruct((bsz, t, d_mb), BF16),
        scratch_shapes=[pltpu.VMEM((HEADS_PER_VREG, n_blocks // 2, SUBLANES, tile), F32)]
        + [pltpu.VMEM((HEADS_PER_VREG, 2 * tile, tile), F32)] * 2
        + [pltpu.VMEM((HEADS_PER_VREG, 2 * tile, tile), BF16)] * 2,
        compiler_params=_cparams(("parallel", "parallel", "arbitrary")),
        name="moba_attn",
    )(q32, x, vt, ksum)


def _ssm_tables(a_re, a_im, log_dt, b_re, b_im, c_re, c_im):
    ng, ns = a_re.shape
    nc = SSM_GROUP_CH
    L = SSM_CHUNK
    a_re, a_im = a_re.astype(F32), a_im.astype(F32)
    dt = jnp.exp(log_dt.astype(F32))[:, None]
    tau = jnp.arange(L + 1, dtype=F32)[:, None, None]
    mag = jnp.exp(tau * (a_re * dt))
    ang = tau * (a_im * dt)
    pw_re, pw_im = mag * jnp.cos(ang), mag * jnp.sin(ang)
    lam_re, lam_im = pw_re[1], pw_im[1]
    den = a_re * a_re + a_im * a_im
    num_re, num_im = lam_re - 1.0, lam_im
    coef_re = (num_re * a_re + num_im * a_im) / den
    coef_im = (num_im * a_re - num_re * a_im) / den
    b_re, b_im = b_re.astype(F32), b_im.astype(F32)
    bb_re = coef_re[..., None] * b_re - coef_im[..., None] * b_im
    bb_im = coef_re[..., None] * b_im + coef_im[..., None] * b_re
    c_re, c_im = c_re.astype(F32), c_im.astype(F32)
    x_re = pw_re[..., None] * bb_re - pw_im[..., None] * bb_im
    x_im = pw_re[..., None] * bb_im + pw_im[..., None] * bb_re
    kern = (jnp.einsum('gcp,tgpd->tgcd', c_re, x_re, precision=_HI)
            - jnp.einsum('gcp,tgpd->tgcd', c_im, x_im, precision=_HI))
    eye = jnp.eye(ng, dtype=F32)
    blockdiag = lambda w: jnp.einsum('gab,gh->gahb', w, eye).reshape(ng * w.shape[1], ng * w.shape[2])
    k_lag = jnp.einsum('tgcd,gh->tgdhc', kern[:L], eye).reshape(L, ng * nc, ng * nc)
    return dict(lam_re=lam_re, lam_im=lam_im,
                k_lag=k_lag.astype(BF16),
                wb_re=blockdiag(jnp.swapaxes(bb_re, 1, 2)),
                wb_im=blockdiag(jnp.swapaxes(bb_im, 1, 2)),
                wc_re=blockdiag(jnp.swapaxes(c_re, 1, 2)),
                wc_im=blockdiag(jnp.swapaxes(c_im, 1, 2)),
                pw_re=pw_re.reshape(L + 1, ng * ns), pw_im=pw_im.reshape(L + 1, ng * ns))


def _ssm_rows_kernel(u_ref, klag_ref, wbr_ref, wbi_ref, wcr_ref, wci_ref, pwr_ref, pwi_ref, d_ref,
                     y_ref, hre_ref, him_ref, sre_ref, sim_ref, pre_ref, pim_ref, cre_ref, cim_ref,
                     *half_refs, n_chunks):
    L = SSM_CHUNK
    i = pl.program_id(1)

    @pl.when(i == 0)
    def _():
        cre_ref[...] = jnp.zeros_like(cre_ref)
        cim_ref[...] = jnp.zeros_like(cim_ref)

    n_half = len(half_refs) // 2
    u_half, y_half = half_refs[:n_half], half_refs[n_half:]
    for k in range(n_half):
        u_half[k][...] = u_ref[:, k * LANES:(k + 1) * LANES]
    us = [jnp.concatenate([u_half[k][pl.ds(s, n_chunks, stride=L), :] for k in range(n_half)],
                          axis=1) for s in range(L)]
    ub = [x.astype(BF16) for x in us]

    s_re = s_im = None
    for s in range(L):
        br, bi = _dot(ub[s], wbr_ref[...]), _dot(ub[s], wbi_ref[...])
        lr, li = pwr_ref[L - 1 - s:L - s, :], pwi_ref[L - 1 - s:L - s, :]
        tr, ti = lr * br - li * bi, lr * bi + li * br
        s_re, s_im = (tr, ti) if s_re is None else (s_re + tr, s_im + ti)
    sre_ref[...] = s_re
    sim_ref[...] = s_im

    lr, li = pwr_ref[L:L + 1, :], pwi_ref[L:L + 1, :]

    def chunk(c, st):
        hr, hi = st
        pre_ref[pl.ds(c, 1), :] = hr
        pim_ref[pl.ds(c, 1), :] = hi
        return (lr * hr - li * hi + sre_ref[pl.ds(c, 1), :],
                lr * hi + li * hr + sim_ref[pl.ds(c, 1), :])

    hr, hi = lax.fori_loop(0, n_chunks, chunk, (cre_ref[...], cim_ref[...]))
    cre_ref[...] = hr
    cim_ref[...] = hi
    hre_ref[...] = hr
    him_ref[...] = hi

    lr, li = pwr_ref[1:2, :], pwi_ref[1:2, :]
    g_re, g_im = pre_ref[...], pim_ref[...]
    for t in range(L):
        g_re, g_im = lr * g_re - li * g_im, lr * g_im + li * g_re
        y = (_dot(g_re.astype(BF16), wcr_ref[...]) - _dot(g_im.astype(BF16), wci_ref[...])
             + d_ref[...] * us[t])
        for s in range(t + 1):
            y = y + _dot(ub[s], klag_ref[t - s])
        for k in range(n_half):
            y_half[k][pl.ds(t, n_chunks, stride=L), :] = y[:, k * LANES:(k + 1) * LANES]
    for k in range(n_half):
        y_ref[:, k * LANES:(k + 1) * LANES] = y_half[k][...]


def _ssm_prompt(u, tabs, ssm_d, bsz, t, n_groups, n_state):
    L = SSM_CHUNK
    n, dc = u.shape
    ds = n_groups * n_state
    rows = min(t, SSM_ROWS)
    assert t % rows == 0 and rows % (L * SUBLANES) == 0
    n_chunks = rows // L
    tiles = t // rows
    bf = lambda a: a.astype(BF16)
    state = pl.BlockSpec((None, 1, ds), lambda b, i: (b, 0, 0))
    y, h_re, h_im = pl.pallas_call(
        functools.partial(_ssm_rows_kernel, n_chunks=n_chunks),
        grid=(bsz, tiles),
        in_specs=[pl.BlockSpec((rows, dc), lambda b, i: (b * tiles + i, 0)),
                  _resident((L, dc, dc)), _resident((dc, ds)), _resident((dc, ds)),
                  _resident((ds, dc)), _resident((ds, dc)),
                  _resident((L + 1, ds)), _resident((L + 1, ds)), _resident((1, dc))],
        out_specs=[pl.BlockSpec((rows, dc), lambda b, i: (b * tiles + i, 0)), state, state],
        out_shape=[jax.ShapeDtypeStruct((n, dc), F32),
                   jax.ShapeDtypeStruct((bsz, 1, ds), F32), jax.ShapeDtypeStruct((bsz, 1, ds), F32)],
        scratch_shapes=[pltpu.VMEM((n_chunks, ds), F32)] * 4 + [pltpu.VMEM((1, ds), F32)] * 2
        + [pltpu.VMEM((rows, LANES), F32)] * (2 * (dc // LANES)),
        compiler_params=_cparams(("parallel", "arbitrary")),
        name="ssm_rows",
    )(u, tabs['k_lag'], bf(tabs['wb_re']), bf(tabs['wb_im']), bf(tabs['wc_re']), bf(tabs['wc_im']),
      tabs['pw_re'], tabs['pw_im'], ssm_d.astype(F32).reshape(1, dc))
    return y, h_re.reshape(bsz, n_groups, n_state), h_im.reshape(bsz, n_groups, n_state)


def _ssm_step_kernel(u_ref, hr_ref, hi_ref, wbr_ref, wbi_ref, lr_ref, li_ref,
                     wcr_ref, wci_ref, d_ref, y_ref, hro_ref, hio_ref):
    u = u_ref[...]
    dot = lambda a, b: jnp.dot(a, b, precision=_HI, preferred_element_type=F32)
    h0r, h0i = hr_ref[...], hi_ref[...]
    lr, li = lr_ref[...], li_ref[...]
    hr = dot(u, wbr_ref[...]) + (lr * h0r - li * h0i)
    hi = dot(u, wbi_ref[...]) + (lr * h0i + li * h0r)
    hro_ref[...] = hr
    hio_ref[...] = hi
    y_ref[...] = dot(hr, wcr_ref[...]) - dot(hi, wci_ref[...]) + d_ref[...] * u


def _ssm_step(u, h0_re, h0_im, tabs, ssm_d):
    bsz, ng, ns = h0_re.shape
    nc = SSM_GROUP_CH
    wb_re, wb_im, wc_re, wc_im = tabs['wb_re'], tabs['wb_im'], tabs['wc_re'], tabs['wc_im']
    flat = lambda a: a.astype(F32).reshape(1, ng * ns)
    y, hr, hi = pl.pallas_call(
        _ssm_step_kernel,
        out_shape=[jax.ShapeDtypeStruct((bsz, ng * nc), F32),
                   jax.ShapeDtypeStruct((bsz, ng * ns), F32),
                   jax.ShapeDtypeStruct((bsz, ng * ns), F32)],
        compiler_params=pltpu.CompilerParams(vmem_limit_bytes=VMEM_LIMIT),
        name="ssm_step",
    )(u, h0_re.astype(F32).reshape(bsz, ng * ns), h0_im.astype(F32).reshape(bsz, ng * ns),
      wb_re, wb_im, flat(tabs['lam_re']), flat(tabs['lam_im']), wc_re, wc_im,
      ssm_d.astype(F32).reshape(1, ng * nc))
    return y, hr.reshape(bsz, ng, ns), hi.reshape(bsz, ng, ns)


def _merge_kernel(x_ref, osb_ref, omb_ref, yssm_ref, wg_ref, bg_ref, wglu_ref,
                  wbsb_ref, wbmb_ref, wbssm_ref, wout_ref, g_ref, b_ref, o_ref, *, alpha):
    x = x_ref[...]
    d = x.shape[1]
    gates = jax.nn.sigmoid(_dot(x.astype(BF16), wg_ref[...]) + bg_ref[...])
    glu = _dot(yssm_ref[...].astype(BF16), wglu_ref[...])
    half = glu.shape[1] // 2
    o_ssm = glu[:, :half] * jax.nn.sigmoid(glu[:, half:])
    merged = (gates[:, 0:d] * _dot(osb_ref[...], wbsb_ref[...])
              + gates[:, d:2 * d] * _dot(omb_ref[...], wbmb_ref[...])
              + gates[:, 2 * d:3 * d] * _dot(o_ssm.astype(BF16), wbssm_ref[...]))
    mix = _dot(merged.astype(BF16), wout_ref[...])
    o_ref[...] = _layer_norm(alpha * x + mix, g_ref[...], b_ref[...])


def _merge(x, o_sb, o_mb, y_ssm, lw, alpha, tm):
    n, d = x.shape
    row = lambda a: pl.BlockSpec((tm, a.shape[1]), lambda i: (i, 0))
    acts = (x, o_sb, o_mb, y_ssm)
    weights = (lw['w_gate'], lw['b_gate'], lw['w_glu'], lw['w_br_sb'], lw['w_br_mb'],
               lw['w_br_ssm'], lw['w_out'], lw['ln2_g'], lw['ln2_b'])
    return pl.pallas_call(
        functools.partial(_merge_kernel, alpha=alpha),
        grid=(n // tm,),
        in_specs=[row(a) for a in acts] + [_resident(w.shape) for w in weights],
        out_specs=pl.BlockSpec((tm, d), lambda i: (i, 0)),
        out_shape=jax.ShapeDtypeStruct((n, d), F32),
        compiler_params=_cparams(("parallel",)),
        name="merge",
    )(*acts, *weights)


def _pages_keys_minor(cache):
    return jnp.transpose(cache, (0, 1, 3, 4, 2))


def _sb_decode_kernel(pt_ref, q_ref, ck_ref, cv_ref, o_ref, kbuf, vbuf, sem,
                      *, n_heads, n_pages, layer):
    b = pl.program_id(0)
    page = kbuf.shape[-1]
    d = n_heads * HEAD_DIM
    assert n_heads <= SUBLANES

    def page_copies(j, slot):
        pg = pt_ref[b, n_pages - 1 - j]
        return (pltpu.make_async_copy(ck_ref.at[layer, pg], kbuf.at[slot], sem.at[slot, 0]),
                pltpu.make_async_copy(cv_ref.at[layer, pg], vbuf.at[slot], sem.at[slot, 1]))

    def start_page(j, slot):
        for cp in page_copies(j, slot):
            cp.start()

    def wait_page(j, slot):
        for cp in page_copies(j, slot):
            cp.wait()

    q = q_ref[...] * (HEAD_DIM ** -0.5)
    row = lax.broadcasted_iota(jnp.int32, (page, page), 0)
    col = lax.broadcasted_iota(jnp.int32, (page, page), 1)
    tri = (row >= col).astype(BF16)
    head_rows = lax.broadcasted_iota(jnp.int32, (SUBLANES, 1), 0) < n_heads
    zero_rows = jnp.zeros((SUBLANES - n_heads, page), F32)

    def alive(carry):
        return jnp.max(jnp.where(head_rows, carry, -jnp.inf)) > F32_EXP_ZERO

    def cond(st):
        j, carry, _ = st
        return jnp.logical_and(j < n_pages, alive(carry))

    def body(st):
        j, carry, acc = st
        slot = j % 2

        @pl.when(j + 1 < n_pages)
        def _():
            start_page(j + 1, 1 - slot)

        wait_page(j, slot)
        z = jnp.concatenate(
            [jnp.sum(kbuf[slot, h] * q[h * HEAD_DIM:(h + 1) * HEAD_DIM], axis=0, keepdims=True)
             for h in range(n_heads)] + [zero_rows], axis=0)
        lk = _neg_softplus(z)
        hi, lo = _split_bf16(lk)
        rcs = _dot(hi, tri) + _dot(lo, tri)
        w = jnp.exp(z + carry + rcs)
        acc = acc + jnp.concatenate(
            [jnp.sum(vbuf[slot, h] * w[h:h + 1, :], axis=-1, keepdims=True)
             for h in range(n_heads)], axis=0)
        return j + 1, carry + rcs[:, 0:1], acc

    start_page(0, 0)
    j, _, acc = lax.while_loop(
        cond, body, (jnp.int32(0), jnp.zeros((SUBLANES, 1), F32), jnp.zeros((d, 1), F32)))

    @pl.when(j < n_pages)
    def _():
        wait_page(j, j % 2)

    o_ref[...] = acc


def _sb_decode(q, cache_k, cache_v, layer, page_table, n_heads):
    bsz, n_pages = page_table.shape
    page = cache_k.shape[2]
    d = n_heads * HEAD_DIM
    ck, cv = _pages_keys_minor(cache_k), _pages_keys_minor(cache_v)
    col = pl.BlockSpec((None, d, 1), lambda b, pt: (b, 0, 0))
    grid_spec = pltpu.PrefetchScalarGridSpec(
        num_scalar_prefetch=1,
        grid=(bsz,),
        in_specs=[col, pl.BlockSpec(memory_space=pl.ANY), pl.BlockSpec(memory_space=pl.ANY)],
        out_specs=col,
        scratch_shapes=[pltpu.VMEM((2, n_heads, HEAD_DIM, page), F32),
                        pltpu.VMEM((2, n_heads, HEAD_DIM, page), F32),
                        pltpu.SemaphoreType.DMA((2, 2))],
    )
    out = pl.pallas_call(
        functools.partial(_sb_decode_kernel, n_heads=n_heads, n_pages=n_pages, layer=layer),
        grid_spec=grid_spec,
        out_shape=jax.ShapeDtypeStruct((bsz, d, 1), F32),
        compiler_params=_cparams(("arbitrary",)),
        name="sb_decode",
    )(page_table, q.reshape(bsz, d, 1), ck, cv)
    return out.reshape(bsz, d).astype(BF16)


def _moba_ksum_kernel(pt_ref, *refs, pages_per_step, pages_per_block):
    k_refs = refs[:pages_per_step]
    o_ref = refs[pages_per_step]
    j = pl.program_id(1)
    bps = pages_per_step // pages_per_block

    @pl.when(j == 0)
    def _():
        o_ref[...] = jnp.zeros_like(o_ref)

    lane = lax.broadcasted_iota(jnp.int32, o_ref.shape, 1)
    out = o_ref[...]
    for m in range(bps):
        s = k_refs[m * pages_per_block][...]
        for r in range(1, pages_per_block):
            s = s + k_refs[m * pages_per_block + r][...]
        col = jnp.sum(s.reshape(-1, s.shape[-1]), axis=-1, keepdims=True)
        out = jnp.where(lane == j * bps + m, col, out)
    o_ref[...] = out


def _moba_ksum(cache_k, layer, page_table, n_heads):
    bsz, n_pages = page_table.shape
    page = cache_k.shape[2]
    d = n_heads * HEAD_DIM
    ppb = MOBA_BLOCK // page
    n_blocks = n_pages // ppb
    pps = math.gcd(n_blocks, 8) * ppb
    ck = _pages_keys_minor(cache_k)

    def page_spec(i):
        return pl.BlockSpec((None, None, n_heads, HEAD_DIM, page),
                            lambda b, j, pt: (layer, pt[b, j * pps + i], 0, 0, 0))

    grid_spec = pltpu.PrefetchScalarGridSpec(
        num_scalar_prefetch=1,
        grid=(bsz, n_pages // pps),
        in_specs=[page_spec(i) for i in range(pps)],
        out_specs=pl.BlockSpec((None, d, n_blocks), lambda b, j, pt: (b, 0, 0)),
    )
    return pl.pallas_call(
        functools.partial(_moba_ksum_kernel, pages_per_step=pps, pages_per_block=ppb),
        grid_spec=grid_spec,
        out_shape=jax.ShapeDtypeStruct((bsz, d, n_blocks), F32),
        compiler_params=_cparams(("parallel", "arbitrary")),
        name="moba_ksum",
    )(page_table, *([ck] * pps))


def _moba_topk_kernel(q_ref, ksum_ref, idx_ref, ok_ref, *, n_heads, n_blocks):
    prod = ksum_ref[...] * q_ref[...]
    score = jnp.sum(prod.reshape(n_heads, HEAD_DIM, n_blocks), axis=1) * (1.0 / MOBA_BLOCK)
    blk = lax.broadcasted_iota(jnp.int32, (n_heads, n_blocks), 1)
    lane = lax.broadcasted_iota(jnp.int32, (n_heads, LANES), 1)
    s = score
    idx_out = jnp.zeros((n_heads, LANES), jnp.int32)
    ok_out = jnp.zeros((n_heads, LANES), jnp.int32)
    for r in range(MOBA_TOPK):
        m = jnp.max(s, axis=-1, keepdims=True)
        idx = jnp.min(jnp.where(s == m, blk, n_blocks), axis=-1, keepdims=True)
        idx_out = jnp.where(lane == r, jnp.minimum(idx, n_blocks - 1), idx_out)
        ok_out = jnp.where(lane == r, (jnp.abs(m) < jnp.inf).astype(jnp.int32), ok_out)
        s = jnp.where(blk == idx, -jnp.inf, s)
    idx_ref[...] = idx_out
    ok_ref[...] = ok_out


def _moba_topk(q, ksum_t, n_heads):
    bsz, d, n_blocks = ksum_t.shape
    assert n_blocks >= MOBA_TOPK
    idx, ok = pl.pallas_call(
        functools.partial(_moba_topk_kernel, n_heads=n_heads, n_blocks=n_blocks),
        grid=(bsz,),
        in_specs=[pl.BlockSpec((None, d, 1), lambda b: (b, 0, 0)),
                  pl.BlockSpec((None, d, n_blocks), lambda b: (b, 0, 0))],
        out_specs=[pl.BlockSpec((None, n_heads, LANES), lambda b: (b, 0, 0))] * 2,
        out_shape=[jax.ShapeDtypeStruct((bsz, n_heads, LANES), jnp.int32)] * 2,
        compiler_params=_cparams(("parallel",)),
        name="moba_topk",
    )(q.reshape(bsz, d, 1), ksum_t)
    return idx[:, :, :MOBA_TOPK], ok[:, :, :MOBA_TOPK]


def _moba_decode_kernel(pt_ref, idx_ref, ok_ref, q_ref, kn_ref, vn_ref, *refs, n_tiles):
    k_refs = refs[:n_tiles]
    v_refs = refs[n_tiles:2 * n_tiles]
    o_ref = refs[2 * n_tiles]
    b = pl.program_id(0)
    h = pl.program_id(1)
    n_heads = pl.num_programs(1)
    ppb = n_tiles // MOBA_TOPK
    q = q_ref[...] * (HEAD_DIM ** -0.5)
    z_own = jnp.sum(q * kn_ref[...], axis=0, keepdims=True)
    zs = []
    for t in range(n_tiles):
        ok = ok_ref[(b * n_heads + h) * MOBA_TOPK + t // ppb] > 0
        z = jnp.sum(k_refs[t][...] * q, axis=0, keepdims=True)
        zs.append(jnp.where(ok, z, NEG_BIG))
    m = z_own
    for z in zs:
        m = jnp.maximum(m, jnp.max(z, axis=-1, keepdims=True))
    p_own = jnp.exp(z_own - m)
    l = p_own
    acc = p_own * vn_ref[...]
    for t in range(n_tiles):
        pw = jnp.exp(zs[t] - m)
        l = l + jnp.sum(pw, axis=-1, keepdims=True)
        acc = acc + jnp.sum(v_refs[t][...] * pw, axis=-1, keepdims=True)
    o_ref[...] = acc / l


def _moba_decode(q, k_new, v_new, idx, ok, cache_k, cache_v, layer, page_table, n_heads):
    bsz, n_pages = page_table.shape
    page = cache_k.shape[2]
    d = n_heads * HEAD_DIM
    ppb = MOBA_BLOCK // page
    n_tiles = MOBA_TOPK * ppb
    ck, cv = _pages_keys_minor(cache_k), _pages_keys_minor(cache_v)

    def tile_spec(t):
        def index_map(b, h, pt, idx, ok):
            blk = idx[(b * n_heads + h) * MOBA_TOPK + t // ppb]
            return (layer, pt[b, blk * ppb + t % ppb], h, 0, 0)
        return pl.BlockSpec((None, None, None, HEAD_DIM, page), index_map)

    tiles = [tile_spec(t) for t in range(n_tiles)]
    col = pl.BlockSpec((None, HEAD_DIM, 1), lambda b, h, pt, idx, ok: (b, h, 0))
    grid_spec = pltpu.PrefetchScalarGridSpec(
        num_scalar_prefetch=3,
        grid=(bsz, n_heads),
        in_specs=[col, col, col] + tiles * 2,
        out_specs=col,
    )
    out = pl.pallas_call(
        functools.partial(_moba_decode_kernel, n_tiles=n_tiles),
        grid_spec=grid_spec,
        out_shape=jax.ShapeDtypeStruct((bsz, d, 1), F32),
        compiler_params=_cparams(("parallel", "parallel")),
        name="moba_decode",
    )(page_table, idx.reshape(-1), ok.reshape(-1), q.reshape(bsz, d, 1), k_new.reshape(bsz, d, 1),
      v_new.reshape(bsz, d, 1), *([ck] * n_tiles), *([cv] * n_tiles))
    return out.reshape(bsz, d).astype(BF16)


def _row_tile(n):
    for tm in (512, 256, 128, 64, 32, 16, 8):
        if n % tm == 0:
            return tm
    return n


def _layer_weights(w, l, d_model, d_sb, d_mb, d_ssm):
    n_qkvu = 3 * d_sb + 3 * d_mb + d_ssm
    bf = lambda a: a.astype(BF16)
    vec = lambda a: a.astype(F32).reshape(1, -1)
    lw = dict(
        w_ffn1_up=bf(w['w_ffn1_up'][l]), w_ffn1_down=bf(w['w_ffn1_down'][l]),
        w_ffn2_up=bf(w['w_ffn2_up'][l]), w_ffn2_down=bf(w['w_ffn2_down'][l]),
        w_qkvu=bf(w['w_in'][l][:, :n_qkvu]), b_qkvu=vec(w['b_in'][l][:n_qkvu]),
        w_gate=bf(w['w_in'][l][:, n_qkvu:]), b_gate=vec(w['b_in'][l][n_qkvu:]),
        w_glu=bf(w['w_glu'][l]), w_br_sb=bf(w['w_br_sb'][l]), w_br_mb=bf(w['w_br_mb'][l]),
        w_br_ssm=bf(w['w_br_ssm'][l]), w_out=bf(w['w_out'][l]),
    )
    for name in ('ln1_g', 'ln1_b', 'ln2_g', 'ln2_b', 'ln3_g', 'ln3_b'):
        lw[name] = vec(w[name][l])
    return lw


def kernel(x_prompt, x_sample, cache_k_sb, cache_v_sb, cache_k_mb, cache_v_mb, state_ssm_re, state_ssm_im, page_table, ln1_g, ln1_b, w_ffn1_up, w_ffn1_down, w_in, b_in, ssm_a_re, ssm_a_im, ssm_log_dt, ssm_b_re, ssm_b_im, ssm_c_re, ssm_c_im, ssm_d, w_glu, w_br_sb, w_br_mb, w_br_ssm, w_out, ln2_g, ln2_b, w_ffn2_up, w_ffn2_down, ln3_g, ln3_b):
    weights = dict(ln1_g=ln1_g, ln1_b=ln1_b, w_ffn1_up=w_ffn1_up, w_ffn1_down=w_ffn1_down,
                   w_in=w_in, b_in=b_in, w_glu=w_glu, w_br_sb=w_br_sb, w_br_mb=w_br_mb,
                   w_br_ssm=w_br_ssm, w_out=w_out, ln2_g=ln2_g, ln2_b=ln2_b,
                   w_ffn2_up=w_ffn2_up, w_ffn2_down=w_ffn2_down, ln3_g=ln3_g, ln3_b=ln3_b)
    depth = w_in.shape[0]
    bsz, t, d_model = x_prompt.shape
    dbsz, dec_t, _ = x_sample.shape
    h_sb, h_mb = cache_k_sb.shape[3], cache_k_mb.shape[3]
    n_groups, n_state = ssm_a_re.shape[1:]
    d_sb, d_mb, d_ssm = h_sb * HEAD_DIM, h_mb * HEAD_DIM, n_groups * SSM_GROUP_CH
    page = cache_k_sb.shape[2]
    past_len = page_table.shape[1] * page
    assert dec_t == 1 and t % MOBA_BLOCK == 0 and t % SSM_CHUNK == 0
    assert MOBA_BLOCK % page == 0 and past_len % MOBA_BLOCK == 0
    assert d_sb % LANES == 0 and d_mb % LANES == 0
    alpha = float((2 * depth) ** 0.25)

    n_p = bsz * t
    tm_p = _row_tile(n_p)
    if tm_p % MOBA_BLOCK:
        tm_p = MOBA_BLOCK
    tm_s = _row_tile(dbsz)

    y_p = x_prompt.reshape(n_p, d_model)
    y_s = x_sample.reshape(dbsz, d_model)
    prompt_states, sample_states = [], []
    prompt_kv = None
    for l in range(depth):
        lw = _layer_weights(weights, l, d_model, d_sb, d_mb, d_ssm)
        tabs = _ssm_tables(ssm_a_re[l], ssm_a_im[l], ssm_log_dt[l], ssm_b_re[l], ssm_b_im[l],
                           ssm_c_re[l], ssm_c_im[l])

        x1 = _ffn_ln(y_p, lw['w_ffn1_up'], lw['w_ffn1_down'], lw['ln1_g'], lw['ln1_b'], alpha, tm_p)
        (*prompt_kv, sb16, kmb16, vmbt16, q_mb32, u, ksum) = _in_proj_prompt(
            x1, lw['w_qkvu'], lw['b_qkvu'], l, depth, bsz, t, d_sb, d_mb, d_ssm, tm_p, prompt_kv)
        o_sb = _sb_attn(sb16, bsz, t, h_sb).reshape(n_p, d_sb)
        o_mb = _moba_attn(kmb16, vmbt16, q_mb32, ksum.reshape(bsz, t // MOBA_BLOCK, d_mb),
                          bsz, t, h_mb).reshape(n_p, d_mb)
        y_ssm, h_re, h_im = _ssm_prompt(u, tabs, ssm_d[l], bsz, t, n_groups, n_state)
        x2 = _merge(x1, o_sb, o_mb, y_ssm, lw, alpha, tm_p)
        y_p = _ffn_ln(x2, lw['w_ffn2_up'], lw['w_ffn2_down'], lw['ln3_g'], lw['ln3_b'], alpha, tm_p)
        prompt_states.append((h_re.astype(state_ssm_re.dtype), h_im.astype(state_ssm_im.dtype)))

        s1 = _ffn_ln(y_s, lw['w_ffn1_up'], lw['w_ffn1_down'], lw['ln1_g'], lw['ln1_b'], alpha, tm_s)
        (k_sb, v_sb, k_mb, v_mb, sb16, mb16, q_mb32, u) = _in_proj(
            s1, lw['w_qkvu'], lw['b_qkvu'], d_sb, d_mb, d_ssm, tm_s)
        q_sb32 = sb16[:, :d_sb].astype(F32) * (HEAD_DIM ** 0.5)
        o_sb = _sb_decode(q_sb32, cache_k_sb, cache_v_sb, l, page_table, h_sb)
        ksum = _moba_ksum(cache_k_mb, l, page_table, h_mb)
        idx, ok = _moba_topk(q_mb32, ksum, h_mb)
        o_mb = _moba_decode(q_mb32, k_mb, v_mb, idx, ok, cache_k_mb, cache_v_mb, l, page_table, h_mb)
        y_ssm, h_re, h_im = _ssm_step(u, state_ssm_re[l], state_ssm_im[l], tabs, ssm_d[l])
        s2 = _merge(s1, o_sb, o_mb, y_ssm, lw, alpha, tm_s)
        y_s = _ffn_ln(s2, lw['w_ffn2_up'], lw['w_ffn2_down'], lw['ln3_g'], lw['ln3_b'], alpha, tm_s)
        sample_states.append((k_sb.reshape(dbsz, 1, h_sb, HEAD_DIM), v_sb.reshape(dbsz, 1, h_sb, HEAD_DIM),
                              k_mb.reshape(dbsz, 1, h_mb, HEAD_DIM), v_mb.reshape(dbsz, 1, h_mb, HEAD_DIM),
                              h_re.astype(state_ssm_re.dtype), h_im.astype(state_ssm_im.dtype)))

    stack = lambda states: tuple(jnp.stack(group) for group in zip(*states))
    h_re_p, h_im_p = stack(prompt_states)
    rows_major = lambda a, h: jnp.transpose(a.reshape(depth, bsz, h, HEAD_DIM, t), (0, 1, 4, 2, 3))
    k_sb_p, v_sb_p = rows_major(prompt_kv[0], h_sb), rows_major(prompt_kv[1], h_sb)
    k_mb_p, v_mb_p = rows_major(prompt_kv[2], h_mb), rows_major(prompt_kv[3], h_mb)
    k_sb_s, v_sb_s, k_mb_s, v_mb_s, h_re_s, h_im_s = stack(sample_states)
    return (y_p.reshape(bsz, t, d_model), y_s.reshape(dbsz, dec_t, d_model),
            k_sb_p, v_sb_p, k_mb_p, v_mb_p, h_re_p, h_im_p,
            k_sb_s, v_sb_s, k_mb_s, v_mb_s, h_re_s, h_im_s)
```

```python
import functools
import math

import jax
import jax.numpy as jnp
from jax import lax
from jax.experimental import pallas as pl
from jax.experimental.pallas import tpu as pltpu

F32 = jnp.float32
BF16 = jnp.bfloat16

HEAD_DIM = 64
SSM_GROUP_CH = 16
MOBA_BLOCK = 256
MOBA_TOPK = 3
LN_EPS = 1e-5
LANES = 128
SUBLANES = 8
HEADS_PER_VREG = LANES // HEAD_DIM
SSM_CHUNK = 16
SSM_ROWS = 4096
NEG_BIG = -1e30
LOG2_E = 1.4426950408889634
F32_EXP_ZERO = -104.0
VMEM_LIMIT = 56 * 1024 * 1024

_HI = lax.Precision.HIGHEST


def _cparams(sem):
    return pltpu.CompilerParams(dimension_semantics=sem, vmem_limit_bytes=VMEM_LIMIT)


def _resident(shape):
    return pl.BlockSpec(shape, lambda *_: (0,) * len(shape), pipeline_mode=pl.Buffered(1))


def _dot(a, b):
    return jnp.dot(a, b, preferred_element_type=F32)


def _dot_nt(a, b):
    return lax.dot_general(a, b, (((1,), (1,)), ((), ())), preferred_element_type=F32)


def _layer_norm(y, g, b):
    mu = jnp.mean(y, axis=-1, keepdims=True)
    d = y - mu
    var = jnp.mean(d * d, axis=-1, keepdims=True)
    return d * lax.rsqrt(var + LN_EPS) * g + b


def _neg_softplus(z):
    return -(jnp.maximum(z, 0.0) + jnp.log(1.0 + jnp.exp(-jnp.abs(z))))


def _split_bf16(x):
    hi = x.astype(BF16)
    lo = (x - hi.astype(F32)).astype(BF16)
    return hi, lo


def _ffn_ln_kernel(x_ref, wup_ref, wdn_ref, g_ref, b_ref, o_ref, *, alpha, d_ff, n_chunk):
    x = x_ref[...]
    xb = x.astype(BF16)
    fc = d_ff // n_chunk
    acc = jnp.zeros_like(x)
    for c in range(n_chunk):
        a = c * fc
        gate = _dot(xb, wup_ref[:, a:a + fc])
        up = _dot(xb, wup_ref[:, d_ff + a:d_ff + a + fc])
        h = (gate * jax.nn.sigmoid(gate) * up).astype(BF16)
        acc = acc + _dot(h, wdn_ref[a:a + fc, :])
    o_ref[...] = _layer_norm(alpha * x + 0.5 * acc, g_ref[...], b_ref[...])


def _ffn_ln(x, w_up, w_down, g, b, alpha, tm):
    n, d = x.shape
    d_ff = w_down.shape[0]
    n_chunk = 2 if d_ff % (2 * LANES) == 0 else 1
    return pl.pallas_call(
        functools.partial(_ffn_ln_kernel, alpha=alpha, d_ff=d_ff, n_chunk=n_chunk),
        grid=(n // tm,),
        in_specs=[pl.BlockSpec((tm, d), lambda i: (i, 0)),
                  _resident((d, 2 * d_ff)), _resident((d_ff, d)),
                  _resident((1, d)), _resident((1, d))],
        out_specs=pl.BlockSpec((tm, d), lambda i: (i, 0)),
        out_shape=jax.ShapeDtypeStruct((n, d), F32),
        compiler_params=_cparams(("parallel",)),
        name="ffn_ln",
    )(x, w_up, w_down, g, b)


def _in_proj_kernel(x_ref, w_ref, b_ref, ksb_ref, vsb_ref, kmb_ref, vmb_ref,
                    sb16_ref, mb16_ref, qmb_ref, u_ref, *, d_sb, d_mb, d_ssm):
    r = _dot(x_ref[...].astype(BF16), w_ref[...]) + b_ref[...]
    o = 0
    q_sb = r[:, o:o + d_sb]; o += d_sb
    k_sb = r[:, o:o + d_sb]; o += d_sb
    v_sb = r[:, o:o + d_sb]; o += d_sb
    q_mb = r[:, o:o + d_mb]; o += d_mb
    k_mb = r[:, o:o + d_mb]; o += d_mb
    v_mb = r[:, o:o + d_mb]; o += d_mb
    u = r[:, o:o + d_ssm]
    scale = HEAD_DIM ** -0.5
    ksb_ref[...] = k_sb
    vsb_ref[...] = v_sb
    kmb_ref[...] = k_mb
    vmb_ref[...] = v_mb
    sb16_ref[:, 0:d_sb] = (q_sb * scale).astype(BF16)
    sb16_ref[:, d_sb:2 * d_sb] = k_sb.astype(BF16)
    sb16_ref[:, 2 * d_sb:3 * d_sb] = v_sb.astype(BF16)
    mb16_ref[:, 0:d_mb] = (q_mb * scale).astype(BF16)
    mb16_ref[:, d_mb:2 * d_mb] = k_mb.astype(BF16)
    mb16_ref[:, 2 * d_mb:3 * d_mb] = v_mb.astype(BF16)
    qmb_ref[...] = q_mb
    u_ref[...] = u


def _in_proj(x, w, b, d_sb, d_mb, d_ssm, tm):
    n, d = x.shape
    d_out = w.shape[1]
    row = lambda width: pl.BlockSpec((tm, width), lambda i: (i, 0))
    widths = (d_sb, d_sb, d_mb, d_mb, 3 * d_sb, 3 * d_mb, d_mb, d_ssm)
    dtypes = (F32, F32, F32, F32, BF16, BF16, F32, F32)
    out_specs = [row(wd) for wd in widths]
    out_shape = [jax.ShapeDtypeStruct((n, wd), dt) for wd, dt in zip(widths, dtypes)]
    return pl.pallas_call(
        functools.partial(_in_proj_kernel, d_sb=d_sb, d_mb=d_mb, d_ssm=d_ssm),
        grid=(n // tm,),
        in_specs=[row(d), _resident((d, d_out)), _resident((1, d_out))],
        out_specs=out_specs,
        out_shape=out_shape,
        compiler_params=_cparams(("parallel",)),
        name="in_proj",
    )(x, w, b)


def _in_proj_prompt_kernel(x_ref, w_ref, b_ref, *refs, d_sb, d_mb, d_ssm, n_alias):
    (ksbt_ref, vsbt_ref, kmbt_ref, vmbt_ref, sb16_ref, kmb16_ref, vmbt16_ref,
     qmb_ref, u_ref, ksum_ref) = refs[n_alias:]
    r = _dot(x_ref[...].astype(BF16), w_ref[...]) + b_ref[...]
    o = 0
    q_sb = r[:, o:o + d_sb]; o += d_sb
    k_sb = r[:, o:o + d_sb]; o += d_sb
    v_sb = r[:, o:o + d_sb]; o += d_sb
    q_mb = r[:, o:o + d_mb]; o += d_mb
    k_mb = r[:, o:o + d_mb]; o += d_mb
    v_mb = r[:, o:o + d_mb]; o += d_mb
    u = r[:, o:o + d_ssm]
    scale = HEAD_DIM ** -0.5
    ksbt_ref[...] = k_sb.T
    vsbt_ref[...] = v_sb.T
    kmbt_ref[...] = k_mb.T
    v_mb_t = v_mb.T
    vmbt_ref[...] = v_mb_t
    vmbt16_ref[...] = v_mb_t.astype(BF16)
    sb16_ref[:, 0:d_sb] = (q_sb * scale).astype(BF16)
    sb16_ref[:, d_sb:2 * d_sb] = k_sb.astype(BF16)
    sb16_ref[:, 2 * d_sb:3 * d_sb] = v_sb.astype(BF16)
    kmb16_ref[...] = k_mb.astype(BF16)
    qmb_ref[...] = q_mb
    u_ref[...] = u
    tm = k_mb.shape[0]
    for j in range(tm // MOBA_BLOCK):
        ksum_ref[0, j:j + 1, :] = jnp.sum(
            k_mb[j * MOBA_BLOCK:(j + 1) * MOBA_BLOCK, :], axis=0, keepdims=True)


def _in_proj_prompt(x, w, b, layer, depth, bsz, t, d_sb, d_mb, d_ssm, tm, prev):
    n, d = x.shape
    d_out = w.shape[1]
    tpb = t // tm
    row = lambda width: pl.BlockSpec((tm, width), lambda i: (i, 0))
    kt = lambda width: pl.BlockSpec((None, None, width, tm), lambda i: (layer, i // tpb, 0, i % tpb))
    kt_shape = lambda width: jax.ShapeDtypeStruct((depth, bsz, width, t), F32)
    nblk = tm // MOBA_BLOCK
    out_specs = [kt(d_sb), kt(d_sb), kt(d_mb), kt(d_mb), row(3 * d_sb), row(d_mb),
                 pl.BlockSpec((None, d_mb, tm), lambda i: (i // tpb, 0, i % tpb)),
                 row(d_mb), row(d_ssm), pl.BlockSpec((1, nblk, d_mb), lambda i: (i, 0, 0))]
    out_shape = [kt_shape(d_sb), kt_shape(d_sb), kt_shape(d_mb), kt_shape(d_mb),
                 jax.ShapeDtypeStruct((n, 3 * d_sb), BF16), jax.ShapeDtypeStruct((n, d_mb), BF16),
                 jax.ShapeDtypeStruct((bsz, d_mb, t), BF16),
                 jax.ShapeDtypeStruct((n, d_mb), F32), jax.ShapeDtypeStruct((n, d_ssm), F32),
                 jax.ShapeDtypeStruct((n // tm, nblk, d_mb), F32)]
    in_specs = [row(d), _resident((d, d_out)), _resident((1, d_out))]
    args = [x, w, b]
    aliases = {}
    if prev is not None:
        in_specs += [pl.BlockSpec(memory_space=pl.ANY)] * len(prev)
        aliases = {len(args) + k: k for k in range(len(prev))}
        args += list(prev)
    return pl.pallas_call(
        functools.partial(_in_proj_prompt_kernel, d_sb=d_sb, d_mb=d_mb, d_ssm=d_ssm,
                          n_alias=len(aliases)),
        grid=(n // tm,),
        in_specs=in_specs,
        out_specs=out_specs,
        out_shape=out_shape,
        input_output_aliases=aliases,
        compiler_params=_cparams(("parallel",)),
        name="in_proj_prompt",
    )(*args)


def _sb_attn_kernel(q_ref, k_ref, v_ref, o_ref, *, tile):
    qi = pl.program_id(2)
    q = q_ref[...]
    lane_head = lax.broadcasted_iota(jnp.int32, (tile, LANES), 1) // HEAD_DIM
    row = lax.broadcasted_iota(jnp.int32, (tile, tile), 0)
    col = lax.broadcasted_iota(jnp.int32, (tile, tile), 1)
    tri = (row >= col).astype(BF16)

    heads = range(HEADS_PER_VREG)
    qhs = [jnp.where(lane_head == h, q, jnp.zeros_like(q)) for h in heads]

    def cond(st):
        kt, carries, _ = st
        top = functools.reduce(jnp.maximum, carries)
        return jnp.logical_and(kt >= 0, jnp.max(top) > F32_EXP_ZERO)

    def body(st):
        kt, carries, accs = st
        start = pl.multiple_of(kt * tile, tile)
        k = k_ref[pl.ds(start, tile), :]
        v = v_ref[pl.ds(start, tile), :]
        valid = jnp.logical_or(kt < qi, col < row)
        zs = [_dot_nt(qhs[h], k) for h in heads]
        lks = [jnp.where(valid, _neg_softplus(zs[h]), 0.0) for h in heads]
        parts = [_split_bf16(lks[h]) for h in heads]
        rcss = [_dot(parts[h][0], tri) + _dot(parts[h][1], tri) for h in heads]
        ws = [jnp.where(valid, jnp.exp(zs[h] + carries[h] + rcss[h]), 0.0) for h in heads]
        accs = [accs[h] + _dot(ws[h].astype(BF16), v) for h in heads]
        return kt - 1, [carries[h] + rcss[h][:, 0:1] for h in heads], accs

    init = (qi, [jnp.zeros((tile, 1), F32) for _ in heads],
            [jnp.zeros((tile, LANES), F32) for _ in heads])
    accs = lax.while_loop(cond, body, init)[2]
    out = jnp.where(lane_head == 0, accs[0], accs[1])
    o_ref[...] = out.astype(o_ref.dtype)


def _sb_attn(sb16, bsz, t, n_heads):
    tile = min(256, t)
    d_sb = n_heads * HEAD_DIM
    nlb = d_sb // LANES
    x = sb16.reshape(bsz, t, 3 * d_sb)
    return pl.pallas_call(
        functools.partial(_sb_attn_kernel, tile=tile),
        grid=(bsz, nlb, t // tile),
        in_specs=[pl.BlockSpec((None, tile, LANES), lambda b, p, i: (b, i, p)),
                  pl.BlockSpec((None, t, LANES), lambda b, p, i: (b, 0, nlb + p)),
                  pl.BlockSpec((None, t, LANES), lambda b, p, i: (b, 0, 2 * nlb + p))],
        out_specs=pl.BlockSpec((None, tile, LANES), lambda b, p, i: (b, i, p)),
        out_shape=jax.ShapeDtypeStruct((bsz, t, d_sb), BF16),
        compiler_params=_cparams(("parallel", "parallel", "arbitrary")),
        name="sb_attn",
    )(x, x, x)


def _moba_select_bias(score, blk, n_blocks):
    sel = jnp.zeros(score.shape, jnp.bool_)
    s = score
    for _ in range(min(MOBA_TOPK, n_blocks)):
        m = jnp.max(s, axis=0, keepdims=True)
        idx = jnp.min(jnp.where(s == m, blk, n_blocks), axis=0, keepdims=True)
        pick = blk == idx
        sel = jnp.logical_or(sel, jnp.logical_and(pick, jnp.abs(m) < jnp.inf))
        s = jnp.where(pick, -jnp.inf, s)
    return jnp.where(sel, 0.0, NEG_BIG)


def _moba_attn_kernel(q32_ref, k_ref, vt_ref, ksum_ref, o_ref, bias_ref,
                      za_ref, zb_ref, pa_ref, pb_ref, *, n_blocks):
    qi = pl.program_id(2)
    tile = MOBA_BLOCK
    step = 2 * tile
    qt = q32_ref[...].T
    row_head = lax.broadcasted_iota(jnp.int32, (LANES, tile), 0) // HEAD_DIM
    blk = lax.broadcasted_iota(jnp.int32, (n_blocks, tile), 0)
    ksum = ksum_ref[...]
    scale = HEAD_DIM ** -0.5 * LOG2_E

    heads = range(HEADS_PER_VREG)
    last_pair = n_blocks // 2 - 1
    qhs = [jnp.where(row_head == h, qt, 0.0) for h in heads]
    qts = [(qhs[h] * scale).astype(BF16) for h in heads]

    def put_logits(z_ref, j):
        start = pl.multiple_of(jnp.minimum(j, last_pair) * step, step)
        k2 = k_ref[pl.ds(start, step), :]
        for h in heads:
            z_ref[h] = _dot(k2, qts[h])

    put_logits(za_ref, 0)
    start = pl.multiple_of(qi * tile, tile)
    k_own = k_ref[pl.ds(start, tile), :]
    z_own = [_dot(k_own, qts[h]) for h in heads]

    for h in heads:
        score = jnp.dot(ksum, qhs[h], precision=_HI, preferred_element_type=F32) * (1.0 / MOBA_BLOCK)
        score = jnp.where(blk < qi, score, -jnp.inf)
        bias = _moba_select_bias(score, blk, n_blocks)
        for jp in range(n_blocks // 2):
            bias_ref[h, jp, 0:2, :] = bias[2 * jp:2 * jp + 2]

    key = lax.broadcasted_iota(jnp.int32, (tile, tile), 0)
    qry = lax.broadcasted_iota(jnp.int32, (tile, tile), 1)
    ms, ls, accs = [], [], []
    for h in heads:
        zt = jnp.where(key <= qry, z_own[h], NEG_BIG)
        m = jnp.max(zt, axis=0, keepdims=True)
        p = jnp.exp2(zt - m)
        ms.append(m)
        ls.append(jnp.sum(p, axis=0, keepdims=True))
        accs.append(_dot(vt_ref[h * HEAD_DIM:(h + 1) * HEAD_DIM, pl.ds(start, tile)], p.astype(BF16)))

    def weighted_values(p_ref, j):
        start = pl.multiple_of(jnp.clip(j, 0, last_pair) * step, step)
        return [_dot(vt_ref[h * HEAD_DIM:(h + 1) * HEAD_DIM, pl.ds(start, step)], p_ref[h])
                for h in heads]

    def trip(j, st, z_cur, z_next, p_cur, p_prev):
        ms, ls, accs, corrs = st
        put_logits(z_next, j + 1)
        pvs = weighted_values(p_prev, j - 1)
        out_m, out_l, out_acc, out_corr = [], [], [], []
        for h in heads:
            bias = bias_ref[h, jnp.minimum(j, last_pair), 0:2, :]
            halves = (z_cur[h, :tile], z_cur[h, tile:])
            tops = [jnp.max(halves[r], axis=0, keepdims=True) + bias[r:r + 1] for r in range(2)]
            m_new = jnp.maximum(ms[h], jnp.maximum(tops[0], tops[1]))
            p = jnp.concatenate([jnp.exp2(halves[r] - (m_new - bias[r:r + 1])) for r in range(2)],
                                axis=0)
            p_cur[h] = p.astype(BF16)
            corr = jnp.exp2(ms[h] - m_new)
            out_m.append(m_new)
            out_l.append(ls[h] * corr + jnp.sum(p, axis=0, keepdims=True))
            out_acc.append(accs[h] * corrs[h] + pvs[h])
            out_corr.append(corr)
        return out_m, out_l, out_acc, out_corr

    def two_trips(i, st):
        st = trip(2 * i, st, za_ref, zb_ref, pa_ref, pb_ref)
        return trip(2 * i + 1, st, zb_ref, za_ref, pb_ref, pa_ref)

    pb_ref[...] = jnp.zeros_like(pb_ref)
    ones = [jnp.ones((1, tile), F32) for _ in heads]
    n_trips = (qi + 1) // 2
    n_pairs = n_trips // 2
    st = lax.fori_loop(0, n_pairs, two_trips, (ms, ls, accs, ones))

    def drain(st, p_last, j_last):
        _, ls, accs, corrs = st
        pvs = weighted_values(p_last, j_last)
        return [accs[h] * corrs[h] + pvs[h] for h in heads], ls

    def odd_tail(st):
        return drain(trip(2 * n_pairs, st, za_ref, zb_ref, pa_ref, pb_ref), pa_ref, 2 * n_pairs)

    def even_tail(st):
        return drain(st, pb_ref, 2 * n_pairs - 1)

    accs, ls = lax.cond(n_trips % 2 == 1, odd_tail, even_tail, st)
    out_t = jnp.concatenate([accs[h] / ls[h] for h in heads], axis=0)
    o_ref[...] = out_t.T.astype(o_ref.dtype)


def _moba_attn(k16, vt16, q_mb32, ksum, bsz, t, n_heads):
    tile = MOBA_BLOCK
    d_mb = n_heads * HEAD_DIM
    nlb = d_mb // LANES
    n_blocks = t // tile
    assert n_blocks % 2 == 0
    x = k16.reshape(bsz, t, d_mb)
    vt = vt16
    q32 = q_mb32.reshape(bsz, t, d_mb)
    return pl.pallas_call(
        functools.partial(_moba_attn_kernel, n_blocks=n_blocks),
        grid=(bsz, nlb, n_blocks),
        in_specs=[pl.BlockSpec((None, tile, LANES), lambda b, p, i: (b, i, p)),
                  pl.BlockSpec((None, t, LANES), lambda b, p, i: (b, 0, p)),
                  pl.BlockSpec((None, LANES, t), lambda b, p, i: (b, p, 0)),
                  pl.BlockSpec((None, n_blocks, LANES), lambda b, p, i: (b, 0, p))],
        out_specs=pl.BlockSpec((None, tile, LANES), lambda b, p, i: (b, i, p)),
        out_shape=jax.ShapeDtypeStruct((bsz, t, d_mb), BF16),
        scratch_shapes=[pltpu.VMEM((HEADS_PER_VREG, n_blocks // 2, SUBLANES, tile), F32)]
        + [pltpu.VMEM((HEADS_PER_VREG, 2 * tile, tile), F32)] * 2
        + [pltpu.VMEM((HEADS_PER_VREG, 2 * tile, tile), BF16)] * 2,
        compiler_params=_cparams(("parallel", "parallel", "arbitrary")),
        name="moba_attn",
    )(q32, x, vt, ksum)


def _ssm_tables(a_re, a_im, log_dt, b_re, b_im, c_re, c_im):
    ng, ns = a_re.shape
    nc = SSM_GROUP_CH
    L = SSM_CHUNK
    a_re, a_im = a_re.astype(F32), a_im.astype(F32)
    dt = jnp.exp(log_dt.astype(F32))[:, None]
    tau = jnp.arange(L + 1, dtype=F32)[:, None, None]
    mag = jnp.exp(tau * (a_re * dt))
    ang = tau * (a_im * dt)
    pw_re, pw_im = mag * jnp.cos(ang), mag * jnp.sin(ang)
    lam_re, lam_im = pw_re[1], pw_im[1]
    den = a_re * a_re + a_im * a_im
    num_re, num_im = lam_re - 1.0, lam_im
    coef_re = (num_re * a_re + num_im * a_im) / den
    coef_im = (num_im * a_re - num_re * a_im) / den
    b_re, b_im = b_re.astype(F32), b_im.astype(F32)
    bb_re = coef_re[..., None] * b_re - coef_im[..., None] * b_im
    bb_im = coef_re[..., None] * b_im + coef_im[..., None] * b_re
    c_re, c_im = c_re.astype(F32), c_im.astype(F32)
    x_re = pw_re[..., None] * bb_re - pw_im[..., None] * bb_im
    x_im = pw_re[..., None] * bb_im + pw_im[..., None] * bb_re
    kern = (jnp.einsum('gcp,tgpd->tgcd', c_re, x_re, precision=_HI)
            - jnp.einsum('gcp,tgpd->tgcd', c_im, x_im, precision=_HI))
    eye = jnp.eye(ng, dtype=F32)
    blockdiag = lambda w: jnp.einsum('gab,gh->gahb', w, eye).reshape(ng * w.shape[1], ng * w.shape[2])
    k_lag = jnp.einsum('tgcd,gh->tgdhc', kern[:L], eye).reshape(L, ng * nc, ng * nc)
    return dict(lam_re=lam_re, lam_im=lam_im,
                k_lag=k_lag.astype(BF16),
                wb_re=blockdiag(jnp.swapaxes(bb_re, 1, 2)),
                wb_im=blockdiag(jnp.swapaxes(bb_im, 1, 2)),
                wc_re=blockdiag(jnp.swapaxes(c_re, 1, 2)),
                wc_im=blockdiag(jnp.swapaxes(c_im, 1, 2)),
                pw_re=pw_re.reshape(L + 1, ng * ns), pw_im=pw_im.reshape(L + 1, ng * ns))


def _ssm_rows_kernel(u_ref, klag_ref, wbr_ref, wbi_ref, wcr_ref, wci_ref, pwr_ref, pwi_ref, d_ref,
                     y_ref, hre_ref, him_ref, sre_ref, sim_ref, pre_ref, pim_ref, cre_ref, cim_ref,
                     *half_refs, n_chunks):
    L = SSM_CHUNK
    i = pl.program_id(1)

    @pl.when(i == 0)
    def _():
        cre_ref[...] = jnp.zeros_like(cre_ref)
        cim_ref[...] = jnp.zeros_like(cim_ref)

    n_half = len(half_refs) // 2
    u_half, y_half = half_refs[:n_half], half_refs[n_half:]
    for k in range(n_half):
        u_half[k][...] = u_ref[:, k * LANES:(k + 1) * LANES]
    us = [jnp.concatenate([u_half[k][pl.ds(s, n_chunks, stride=L), :] for k in range(n_half)],
                          axis=1) for s in range(L)]
    ub = [x.astype(BF16) for x in us]

    s_re = s_im = None
    for s in range(L):
        br, bi = _dot(ub[s], wbr_ref[...]), _dot(ub[s], wbi_ref[...])
        lr, li = pwr_ref[L - 1 - s:L - s, :], pwi_ref[L - 1 - s:L - s, :]
        tr, ti = lr * br - li * bi, lr * bi + li * br
        s_re, s_im = (tr, ti) if s_re is None else (s_re + tr, s_im + ti)
    sre_ref[...] = s_re
    sim_ref[...] = s_im

    lr, li = pwr_ref[L:L + 1, :], pwi_ref[L:L + 1, :]

    def chunk(c, st):
        hr, hi = st
        pre_ref[pl.ds(c, 1), :] = hr
        pim_ref[pl.ds(c, 1), :] = hi
        return (lr * hr - li * hi + sre_ref[pl.ds(c, 1), :],
                lr * hi + li * hr + sim_ref[pl.ds(c, 1), :])

    hr, hi = lax.fori_loop(0, n_chunks, chunk, (cre_ref[...], cim_ref[...]))
    cre_ref[...] = hr
    cim_ref[...] = hi
    hre_ref[...] = hr
    him_ref[...] = hi

    lr, li = pwr_ref[1:2, :], pwi_ref[1:2, :]
    g_re, g_im = pre_ref[...], pim_ref[...]
    for t in range(L):
        g_re, g_im = lr * g_re - li * g_im, lr * g_im + li * g_re
        y = (_dot(g_re.astype(BF16), wcr_ref[...]) - _dot(g_im.astype(BF16), wci_ref[...])
             + d_ref[...] * us[t])
        for s in range(t + 1):
            y = y + _dot(ub[s], klag_ref[t - s])
        for k in range(n_half):
            y_half[k][pl.ds(t, n_chunks, stride=L), :] = y[:, k * LANES:(k + 1) * LANES]
    for k in range(n_half):
        y_ref[:, k * LANES:(k + 1) * LANES] = y_half[k][...]


def _ssm_prompt(u, tabs, ssm_d, bsz, t, n_groups, n_state):
    L = SSM_CHUNK
    n, dc = u.shape
    ds = n_groups * n_state
    rows = min(t, SSM_ROWS)
    assert t % rows == 0 and rows % (L * SUBLANES) == 0
    n_chunks = rows // L
    tiles = t // rows
    bf = lambda a: a.astype(BF16)
    state = pl.BlockSpec((None, 1, ds), lambda b, i: (b, 0, 0))
    y, h_re, h_im = pl.pallas_call(
        functools.partial(_ssm_rows_kernel, n_chunks=n_chunks),
        grid=(bsz, tiles),
        in_specs=[pl.BlockSpec((rows, dc), lambda b, i: (b * tiles + i, 0)),
                  _resident((L, dc, dc)), _resident((dc, ds)), _resident((dc, ds)),
                  _resident((ds, dc)), _resident((ds, dc)),
                  _resident((L + 1, ds)), _resident((L + 1, ds)), _resident((1, dc))],
        out_specs=[pl.BlockSpec((rows, dc), lambda b, i: (b * tiles + i, 0)), state, state],
        out_shape=[jax.ShapeDtypeStruct((n, dc), F32),
                   jax.ShapeDtypeStruct((bsz, 1, ds), F32), jax.ShapeDtypeStruct((bsz, 1, ds), F32)],
        scratch_shapes=[pltpu.VMEM((n_chunks, ds), F32)] * 4 + [pltpu.VMEM((1, ds), F32)] * 2
        + [pltpu.VMEM((rows, LANES), F32)] * (2 * (dc // LANES)),
        compiler_params=_cparams(("parallel", "arbitrary")),
        name="ssm_rows",
    )(u, tabs['k_lag'], bf(tabs['wb_re']), bf(tabs['wb_im']), bf(tabs['wc_re']), bf(tabs['wc_im']),
      tabs['pw_re'], tabs['pw_im'], ssm_d.astype(F32).reshape(1, dc))
    return y, h_re.reshape(bsz, n_groups, n_state), h_im.reshape(bsz, n_groups, n_state)


def _ssm_step_kernel(u_ref, hr_ref, hi_ref, wbr_ref, wbi_ref, lr_ref, li_ref,
                     wcr_ref, wci_ref, d_ref, y_ref, hro_ref, hio_ref):
    u = u_ref[...]
    dot = lambda a, b: jnp.dot(a, b, precision=_HI, preferred_element_type=F32)
    h0r, h0i = hr_ref[...], hi_ref[...]
    lr, li = lr_ref[...], li_ref[...]
    hr = dot(u, wbr_ref[...]) + (lr * h0r - li * h0i)
    hi = dot(u, wbi_ref[...]) + (lr * h0i + li * h0r)
    hro_ref[...] = hr
    hio_ref[...] = hi
    y_ref[...] = dot(hr, wcr_ref[...]) - dot(hi, wci_ref[...]) + d_ref[...] * u


def _ssm_step(u, h0_re, h0_im, tabs, ssm_d):
    bsz, ng, ns = h0_re.shape
    nc = SSM_GROUP_CH
    wb_re, wb_im, wc_re, wc_im = tabs['wb_re'], tabs['wb_im'], tabs['wc_re'], tabs['wc_im']
    flat = lambda a: a.astype(F32).reshape(1, ng * ns)
    y, hr, hi = pl.pallas_call(
        _ssm_step_kernel,
        out_shape=[jax.ShapeDtypeStruct((bsz, ng * nc), F32),
                   jax.ShapeDtypeStruct((bsz, ng * ns), F32),
                   jax.ShapeDtypeStruct((bsz, ng * ns), F32)],
        compiler_params=pltpu.CompilerParams(vmem_limit_bytes=VMEM_LIMIT),
        name="ssm_step",
    )(u, h0_re.astype(F32).reshape(bsz, ng * ns), h0_im.astype(F32).reshape(bsz, ng * ns),
      wb_re, wb_im, flat(tabs['lam_re']), flat(tabs['lam_im']), wc_re, wc_im,
      ssm_d.astype(F32).reshape(1, ng * nc))
    return y, hr.reshape(bsz, ng, ns), hi.reshape(bsz, ng, ns)


def _merge_kernel(x_ref, osb_ref, omb_ref, yssm_ref, wg_ref, bg_ref, wglu_ref,
                  wbsb_ref, wbmb_ref, wbssm_ref, wout_ref, g_ref, b_ref, o_ref, *, alpha):
    x = x_ref[...]
    d = x.shape[1]
    gates = jax.nn.sigmoid(_dot(x.astype(BF16), wg_ref[...]) + bg_ref[...])
    glu = _dot(yssm_ref[...].astype(BF16), wglu_ref[...])
    half = glu.shape[1] // 2
    o_ssm = glu[:, :half] * jax.nn.sigmoid(glu[:, half:])
    merged = (gates[:, 0:d] * _dot(osb_ref[...], wbsb_ref[...])
              + gates[:, d:2 * d] * _dot(omb_ref[...], wbmb_ref[...])
              + gates[:, 2 * d:3 * d] * _dot(o_ssm.astype(BF16), wbssm_ref[...]))
    mix = _dot(merged.astype(BF16), wout_ref[...])
    o_ref[...] = _layer_norm(alpha * x + mix, g_ref[...], b_ref[...])


def _merge(x, o_sb, o_mb, y_ssm, lw, alpha, tm):
    n, d = x.shape
    row = lambda a: pl.BlockSpec((tm, a.shape[1]), lambda i: (i, 0))
    acts = (x, o_sb, o_mb, y_ssm)
    weights = (lw['w_gate'], lw['b_gate'], lw['w_glu'], lw['w_br_sb'], lw['w_br_mb'],
               lw['w_br_ssm'], lw['w_out'], lw['ln2_g'], lw['ln2_b'])
    return pl.pallas_call(
        functools.partial(_merge_kernel, alpha=alpha),
        grid=(n // tm,),
        in_specs=[row(a) for a in acts] + [_resident(w.shape) for w in weights],
        out_specs=pl.BlockSpec((tm, d), lambda i: (i, 0)),
        out_shape=jax.ShapeDtypeStruct((n, d), F32),
        compiler_params=_cparams(("parallel",)),
        name="merge",
    )(*acts, *weights)


def _pages_keys_minor(cache):
    return jnp.transpose(cache, (0, 1, 3, 4, 2))


def _sb_decode_kernel(pt_ref, q_ref, ck_ref, cv_ref, o_ref, kbuf, vbuf, sem,
                      *, n_heads, n_pages, layer):
    b = pl.program_id(0)
    page = kbuf.shape[-1]
    d = n_heads * HEAD_DIM
    assert n_heads <= SUBLANES

    def page_copies(j, slot):
        pg = pt_ref[b, n_pages - 1 - j]
        return (pltpu.make_async_copy(ck_ref.at[layer, pg], kbuf.at[slot], sem.at[slot, 0]),
                pltpu.make_async_copy(cv_ref.at[layer, pg], vbuf.at[slot], sem.at[slot, 1]))

    def start_page(j, slot):
        for cp in page_copies(j, slot):
            cp.start()

    def wait_page(j, slot):
        for cp in page_copies(j, slot):
            cp.wait()

    q = q_ref[...] * (HEAD_DIM ** -0.5)
    row = lax.broadcasted_iota(jnp.int32, (page, page), 0)
    col = lax.broadcasted_iota(jnp.int32, (page, page), 1)
    tri = (row >= col).astype(BF16)
    head_rows = lax.broadcasted_iota(jnp.int32, (SUBLANES, 1), 0) < n_heads
    zero_rows = jnp.zeros((SUBLANES - n_heads, page), F32)

    def alive(carry):
        return jnp.max(jnp.where(head_rows, carry, -jnp.inf)) > F32_EXP_ZERO

    def cond(st):
        j, carry, _ = st
        return jnp.logical_and(j < n_pages, alive(carry))

    def body(st):
        j, carry, acc = st
        slot = j % 2

        @pl.when(j + 1 < n_pages)
        def _():
            start_page(j + 1, 1 - slot)

        wait_page(j, slot)
        z = jnp.concatenate(
            [jnp.sum(kbuf[slot, h] * q[h * HEAD_DIM:(h + 1) * HEAD_DIM], axis=0, keepdims=True)
             for h in range(n_heads)] + [zero_rows], axis=0)
        lk = _neg_softplus(z)
        hi, lo = _split_bf16(lk)
        rcs = _dot(hi, tri) + _dot(lo, tri)
        w = jnp.exp(z + carry + rcs)
        acc = acc + jnp.concatenate(
            [jnp.sum(vbuf[slot, h] * w[h:h + 1, :], axis=-1, keepdims=True)
             for h in range(n_heads)], axis=0)
        return j + 1, carry + rcs[:, 0:1], acc

    start_page(0, 0)
    j, _, acc = lax.while_loop(
        cond, body, (jnp.int32(0), jnp.zeros((SUBLANES, 1), F32), jnp.zeros((d, 1), F32)))

    @pl.when(j < n_pages)
    def _():
        wait_page(j, j % 2)

    o_ref[...] = acc


def _sb_decode(q, cache_k, cache_v, layer, page_table, n_heads):
    bsz, n_pages = page_table.shape
    page = cache_k.shape[2]
    d = n_heads * HEAD_DIM
    ck, cv = _pages_keys_minor(cache_k), _pages_keys_minor(cache_v)
    col = pl.BlockSpec((None, d, 1), lambda b, pt: (b, 0, 0))
    grid_spec = pltpu.PrefetchScalarGridSpec(
        num_scalar_prefetch=1,
        grid=(bsz,),
        in_specs=[col, pl.BlockSpec(memory_space=pl.ANY), pl.BlockSpec(memory_space=pl.ANY)],
        out_specs=col,
        scratch_shapes=[pltpu.VMEM((2, n_heads, HEAD_DIM, page), F32),
                        pltpu.VMEM((2, n_heads, HEAD_DIM, page), F32),
                        pltpu.SemaphoreType.DMA((2, 2))],
    )
    out = pl.pallas_call(
        functools.partial(_sb_decode_kernel, n_heads=n_heads, n_pages=n_pages, layer=layer),
        grid_spec=grid_spec,
        out_shape=jax.ShapeDtypeStruct((bsz, d, 1), F32),
        compiler_params=_cparams(("arbitrary",)),
        name="sb_decode",
    )(page_table, q.reshape(bsz, d, 1), ck, cv)
    return out.reshape(bsz, d).astype(BF16)


def _moba_ksum_kernel(pt_ref, *refs, pages_per_step, pages_per_block):
    k_refs = refs[:pages_per_step]
    o_ref = refs[pages_per_step]
    j = pl.program_id(1)
    bps = pages_per_step // pages_per_block

    @pl.when(j == 0)
    def _():
        o_ref[...] = jnp.zeros_like(o_ref)

    lane = lax.broadcasted_iota(jnp.int32, o_ref.shape, 1)
    out = o_ref[...]
    for m in range(bps):
        s = k_refs[m * pages_per_block][...]
        for r in range(1, pages_per_block):
            s = s + k_refs[m * pages_per_block + r][...]
        col = jnp.sum(s.reshape(-1, s.shape[-1]), axis=-1, keepdims=True)
        out = jnp.where(lane == j * bps + m, col, out)
    o_ref[...] = out


def _moba_ksum(cache_k, layer, page_table, n_heads):
    bsz, n_pages = page_table.shape
    page = cache_k.shape[2]
    d = n_heads * HEAD_DIM
    ppb = MOBA_BLOCK // page
    n_blocks = n_pages // ppb
    pps = math.gcd(n_blocks, 8) * ppb
    ck = _pages_keys_minor(cache_k)

    def page_spec(i):
        return pl.BlockSpec((None, None, n_heads, HEAD_DIM, page),
                            lambda b, j, pt: (layer, pt[b, j * pps + i], 0, 0, 0))

    grid_spec = pltpu.PrefetchScalarGridSpec(
        num_scalar_prefetch=1,
        grid=(bsz, n_pages // pps),
        in_specs=[page_spec(i) for i in range(pps)],
        out_specs=pl.BlockSpec((None, d, n_blocks), lambda b, j, pt: (b, 0, 0)),
    )
    return pl.pallas_call(
        functools.partial(_moba_ksum_kernel, pages_per_step=pps, pages_per_block=ppb),
        grid_spec=grid_spec,
        out_shape=jax.ShapeDtypeStruct((bsz, d, n_blocks), F32),
        compiler_params=_cparams(("parallel", "arbitrary")),
        name="moba_ksum",
    )(page_table, *([ck] * pps))


def _moba_topk_kernel(q_ref, ksum_ref, idx_ref, ok_ref, *, n_heads, n_blocks):
    prod = ksum_ref[...] * q_ref[...]
    score = jnp.sum(prod.reshape(n_heads, HEAD_DIM, n_blocks), axis=1) * (1.0 / MOBA_BLOCK)
    blk = lax.broadcasted_iota(jnp.int32, (n_heads, n_blocks), 1)
    lane = lax.broadcasted_iota(jnp.int32, (n_heads, LANES), 1)
    s = score
    idx_out = jnp.zeros((n_heads, LANES), jnp.int32)
    ok_out = jnp.zeros((n_heads, LANES), jnp.int32)
    for r in range(MOBA_TOPK):
        m = jnp.max(s, axis=-1, keepdims=True)
        idx = jnp.min(jnp.where(s == m, blk, n_blocks), axis=-1, keepdims=True)
        idx_out = jnp.where(lane == r, jnp.minimum(idx, n_blocks - 1), idx_out)
        ok_out = jnp.where(lane == r, (jnp.abs(m) < jnp.inf).astype(jnp.int32), ok_out)
        s = jnp.where(blk == idx, -jnp.inf, s)
    idx_ref[...] = idx_out
    ok_ref[...] = ok_out


def _moba_topk(q, ksum_t, n_heads):
    bsz, d, n_blocks = ksum_t.shape
    assert n_blocks >= MOBA_TOPK
    idx, ok = pl.pallas_call(
        functools.partial(_moba_topk_kernel, n_heads=n_heads, n_blocks=n_blocks),
        grid=(bsz,),
        in_specs=[pl.BlockSpec((None, d, 1), lambda b: (b, 0, 0)),
                  pl.BlockSpec((None, d, n_blocks), lambda b: (b, 0, 0))],
        out_specs=[pl.BlockSpec((None, n_heads, LANES), lambda b: (b, 0, 0))] * 2,
        out_shape=[jax.ShapeDtypeStruct((bsz, n_heads, LANES), jnp.int32)] * 2,
        compiler_params=_cparams(("parallel",)),
        name="moba_topk",
    )(q.reshape(bsz, d, 1), ksum_t)
    return idx[:, :, :MOBA_TOPK], ok[:, :, :MOBA_TOPK]


def _moba_decode_kernel(pt_ref, idx_ref, ok_ref, q_ref, kn_ref, vn_ref, *refs, n_tiles):
    k_refs = refs[:n_tiles]
    v_refs = refs[n_tiles:2 * n_tiles]
    o_ref = refs[2 * n_tiles]
    b = pl.program_id(0)
    h = pl.program_id(1)
    n_heads = pl.num_programs(1)
    ppb = n_tiles // MOBA_TOPK
    q = q_ref[...] * (HEAD_DIM ** -0.5)
    z_own = jnp.sum(q * kn_ref[...], axis=0, keepdims=True)
    zs = []
    for t in range(n_tiles):
        ok = ok_ref[(b * n_heads + h) * MOBA_TOPK + t // ppb] > 0
        z = jnp.sum(k_refs[t][...] * q, axis=0, keepdims=True)
        zs.append(jnp.where(ok, z, NEG_BIG))
    m = z_own
    for z in zs:
        m = jnp.maximum(m, jnp.max(z, axis=-1, keepdims=True))
    p_own = jnp.exp(z_own - m)
    l = p_own
    acc = p_own * vn_ref[...]
    for t in range(n_tiles):
        pw = jnp.exp(zs[t] - m)
        l = l + jnp.sum(pw, axis=-1, keepdims=True)
        acc = acc + jnp.sum(v_refs[t][...] * pw, axis=-1, keepdims=True)
    o_ref[...] = acc / l


def _moba_decode(q, k_new, v_new, idx, ok, cache_k, cache_v, layer, page_table, n_heads):
    bsz, n_pages = page_table.shape
    page = cache_k.shape[2]
    d = n_heads * HEAD_DIM
    ppb = MOBA_BLOCK // page
    n_tiles = MOBA_TOPK * ppb
    ck, cv = _pages_keys_minor(cache_k), _pages_keys_minor(cache_v)

    def tile_spec(t):
        def index_map(b, h, pt, idx, ok):
            blk = idx[(b * n_heads + h) * MOBA_TOPK + t // ppb]
            return (layer, pt[b, blk * ppb + t % ppb], h, 0, 0)
        return pl.BlockSpec((None, None, None, HEAD_DIM, page), index_map)

    tiles = [tile_spec(t) for t in range(n_tiles)]
    col = pl.BlockSpec((None, HEAD_DIM, 1), lambda b, h, pt, idx, ok: (b, h, 0))
    grid_spec = pltpu.PrefetchScalarGridSpec(
        num_scalar_prefetch=3,
        grid=(bsz, n_heads),
        in_specs=[col, col, col] + tiles * 2,
        out_specs=col,
    )
    out = pl.pallas_call(
        functools.partial(_moba_decode_kernel, n_tiles=n_tiles),
        grid_spec=grid_spec,
        out_shape=jax.ShapeDtypeStruct((bsz, d, 1), F32),
        compiler_params=_cparams(("parallel", "parallel")),
        name="moba_decode",
    )(page_table, idx.reshape(-1), ok.reshape(-1), q.reshape(bsz, d, 1), k_new.reshape(bsz, d, 1),
      v_new.reshape(bsz, d, 1), *([ck] * n_tiles), *([cv] * n_tiles))
    return out.reshape(bsz, d).astype(BF16)


def _row_tile(n):
    for tm in (512, 256, 128, 64, 32, 16, 8):
        if n % tm == 0:
            return tm
    return n


def _layer_weights(w, l, d_model, d_sb, d_mb, d_ssm):
    n_qkvu = 3 * d_sb + 3 * d_mb + d_ssm
    bf = lambda a: a.astype(BF16)
    vec = lambda a: a.astype(F32).reshape(1, -1)
    lw = dict(
        w_ffn1_up=bf(w['w_ffn1_up'][l]), w_ffn1_down=bf(w['w_ffn1_down'][l]),
        w_ffn2_up=bf(w['w_ffn2_up'][l]), w_ffn2_down=bf(w['w_ffn2_down'][l]),
        w_qkvu=bf(w['w_in'][l][:, :n_qkvu]), b_qkvu=vec(w['b_in'][l][:n_qkvu]),
        w_gate=bf(w['w_in'][l][:, n_qkvu:]), b_gate=vec(w['b_in'][l][n_qkvu:]),
        w_glu=bf(w['w_glu'][l]), w_br_sb=bf(w['w_br_sb'][l]), w_br_mb=bf(w['w_br_mb'][l]),
        w_br_ssm=bf(w['w_br_ssm'][l]), w_out=bf(w['w_out'][l]),
    )
    for name in ('ln1_g', 'ln1_b', 'ln2_g', 'ln2_b', 'ln3_g', 'ln3_b'):
        lw[name] = vec(w[name][l])
    return lw


def kernel(x_prompt, x_sample, cache_k_sb, cache_v_sb, cache_k_mb, cache_v_mb, state_ssm_re, state_ssm_im, page_table, ln1_g, ln1_b, w_ffn1_up, w_ffn1_down, w_in, b_in, ssm_a_re, ssm_a_im, ssm_log_dt, ssm_b_re, ssm_b_im, ssm_c_re, ssm_c_im, ssm_d, w_glu, w_br_sb, w_br_mb, w_br_ssm, w_out, ln2_g, ln2_b, w_ffn2_up, w_ffn2_down, ln3_g, ln3_b):
    weights = dict(ln1_g=ln1_g, ln1_b=ln1_b, w_ffn1_up=w_ffn1_up, w_ffn1_down=w_ffn1_down,
                   w_in=w_in, b_in=b_in, w_glu=w_glu, w_br_sb=w_br_sb, w_br_mb=w_br_mb,
                   w_br_ssm=w_br_ssm, w_out=w_out, ln2_g=ln2_g, ln2_b=ln2_b,
                   w_ffn2_up=w_ffn2_up, w_ffn2_down=w_ffn2_down, ln3_g=ln3_g, ln3_b=ln3_b)
    depth = w_in.shape[0]
    bsz, t, d_model = x_prompt.shape
    dbsz, dec_t, _ = x_sample.shape
    h_sb, h_mb = cache_k_sb.shape[3], cache_k_mb.shape[3]
    n_groups, n_state = ssm_a_re.shape[1:]
    d_sb, d_mb, d_ssm = h_sb * HEAD_DIM, h_mb * HEAD_DIM, n_groups * SSM_GROUP_CH
    page = cache_k_sb.shape[2]
    past_len = page_table.shape[1] * page
    assert dec_t == 1 and t % MOBA_BLOCK == 0 and t % SSM_CHUNK == 0
    assert MOBA_BLOCK % page == 0 and past_len % MOBA_BLOCK == 0
    assert d_sb % LANES == 0 and d_mb % LANES == 0
    alpha = float((2 * depth) ** 0.25)

    n_p = bsz * t
    tm_p = _row_tile(n_p)
    if tm_p % MOBA_BLOCK:
        tm_p = MOBA_BLOCK
    tm_s = _row_tile(dbsz)

    y_p = x_prompt.reshape(n_p, d_model)
    y_s = x_sample.reshape(dbsz, d_model)
    prompt_states, sample_states = [], []
    prompt_kv = None
    for l in range(depth):
        lw = _layer_weights(weights, l, d_model, d_sb, d_mb, d_ssm)
        tabs = _ssm_tables(ssm_a_re[l], ssm_a_im[l], ssm_log_dt[l], ssm_b_re[l], ssm_b_im[l],
                           ssm_c_re[l], ssm_c_im[l])

        x1 = _ffn_ln(y_p, lw['w_ffn1_up'], lw['w_ffn1_down'], lw['ln1_g'], lw['ln1_b'], alpha, tm_p)
        (*prompt_kv, sb16, kmb16, vmbt16, q_mb32, u, ksum) = _in_proj_prompt(
            x1, lw['w_qkvu'], lw['b_qkvu'], l, depth, bsz, t, d_sb, d_mb, d_ssm, tm_p, prompt_kv)
        o_sb = _sb_attn(sb16, bsz, t, h_sb).reshape(n_p, d_sb)
        o_mb = _moba_attn(kmb16, vmbt16, q_mb32, ksum.reshape(bsz, t // MOBA_BLOCK, d_mb),
                          bsz, t, h_mb).reshape(n_p, d_mb)
        y_ssm, h_re, h_im = _ssm_prompt(u, tabs, ssm_d[l], bsz, t, n_groups, n_state)
        x2 = _merge(x1, o_sb, o_mb, y_ssm, lw, alpha, tm_p)
        y_p = _ffn_ln(x2, lw['w_ffn2_up'], lw['w_ffn2_down'], lw['ln3_g'], lw['ln3_b'], alpha, tm_p)
        prompt_states.append((h_re.astype(state_ssm_re.dtype), h_im.astype(state_ssm_im.dtype)))

        s1 = _ffn_ln(y_s, lw['w_ffn1_up'], lw['w_ffn1_down'], lw['ln1_g'], lw['ln1_b'], alpha, tm_s)
        (k_sb, v_sb, k_mb, v_mb, sb16, mb16, q_mb32, u) = _in_proj(
            s1, lw['w_qkvu'], lw['b_qkvu'], d_sb, d_mb, d_ssm, tm_s)
        q_sb32 = sb16[:, :d_sb].astype(F32) * (HEAD_DIM ** 0.5)
        o_sb = _sb_decode(q_sb32, cache_k_sb, cache_v_sb, l, page_table, h_sb)
        ksum = _moba_ksum(cache_k_mb, l, page_table, h_mb)
        idx, ok = _moba_topk(q_mb32, ksum, h_mb)
        o_mb = _moba_decode(q_mb32, k_mb, v_mb, idx, ok, cache_k_mb, cache_v_mb, l, page_table, h_mb)
        y_ssm, h_re, h_im = _ssm_step(u, state_ssm_re[l], state_ssm_im[l], tabs, ssm_d[l])
        s2 = _merge(s1, o_sb, o_mb, y_ssm, lw, alpha, tm_s)
        y_s = _ffn_ln(s2, lw['w_ffn2_up'], lw['w_ffn2_down'], lw['ln3_g'], lw['ln3_b'], alpha, tm_s)
        sample_states.append((k_sb.reshape(dbsz, 1, h_sb, HEAD_DIM), v_sb.reshape(dbsz, 1, h_sb, HEAD_DIM),
                              k_mb.reshape(dbsz, 1, h_mb, HEAD_DIM), v_mb.reshape(dbsz, 1, h_mb, HEAD_DIM),
                              h_re.astype(state_ssm_re.dtype), h_im.astype(state_ssm_im.dtype)))

    stack = lambda states: tuple(jnp.stack(group) for group in zip(*states))
    h_re_p, h_im_p = stack(prompt_states)
    rows_major = lambda a, h: jnp.transpose(a.reshape(depth, bsz, h, HEAD_DIM, t), (0, 1, 4, 2, 3))
    k_sb_p, v_sb_p = rows_major(prompt_kv[0], h_sb), rows_major(prompt_kv[1], h_sb)
    k_mb_p, v_mb_p = rows_major(prompt_kv[2], h_mb), rows_major(prompt_kv[3], h_mb)
    k_sb_s, v_sb_s, k_mb_s, v_mb_s, h_re_s, h_im_s = stack(sample_states)
    return (y_p.reshape(bsz, t, d_model), y_s.reshape(dbsz, dec_t, d_model),
            k_sb_p, v_sb_p, k_mb_p, v_mb_p, h_re_p, h_im_p,
            k_sb_s, v_sb_s, k_mb_s, v_mb_s, h_re_s, h_im_s)
```

```python
import functools
import math

import jax
import jax.numpy as jnp
from jax import lax
from jax.experimental import pallas as pl
from jax.experimental.pallas import tpu as pltpu

F32 = jnp.float32
BF16 = jnp.bfloat16

HEAD_DIM = 64
SSM_GROUP_CH = 16
MOBA_BLOCK = 256
MOBA_TOPK = 3
LN_EPS = 1e-5
LANES = 128
SUBLANES = 8
HEADS_PER_VREG = LANES // HEAD_DIM
SSM_CHUNK = 16
SSM_ROWS = 4096
NEG_BIG = -1e30
LOG2_E = 1.4426950408889634
F32_EXP_ZERO = -104.0
VMEM_LIMIT = 56 * 1024 * 1024

_HI = lax.Precision.HIGHEST


def _cparams(sem):
    return pltpu.CompilerParams(dimension_semantics=sem, vmem_limit_bytes=VMEM_LIMIT)


def _resident(shape):
    return pl.BlockSpec(shape, lambda *_: (0,) * len(shape), pipeline_mode=pl.Buffered(1))


def _dot(a, b):
    return jnp.dot(a, b, preferred_element_type=F32)


def _dot_nt(a, b):
    return lax.dot_general(a, b, (((1,), (1,)), ((), ())), preferred_element_type=F32)


def _layer_norm(y, g, b):
    mu = jnp.mean(y, axis=-1, keepdims=True)
    d = y - mu
    var = jnp.mean(d * d, axis=-1, keepdims=True)
    return d * lax.rsqrt(var + LN_EPS) * g + b


def _neg_softplus(z):
    return -(jnp.maximum(z, 0.0) + jnp.log(1.0 + jnp.exp(-jnp.abs(z))))


def _split_bf16(x):
    hi = x.astype(BF16)
    lo = (x - hi.astype(F32)).astype(BF16)
    return hi, lo


def _ffn_ln_kernel(x_ref, wup_ref, wdn_ref, g_ref, b_ref, o_ref, *, alpha, d_ff, n_chunk):
    x = x_ref[...]
    xb = x.astype(BF16)
    fc = d_ff // n_chunk
    acc = jnp.zeros_like(x)
    for c in range(n_chunk):
        a = c * fc
        gate = _dot(xb, wup_ref[:, a:a + fc])
        up = _dot(xb, wup_ref[:, d_ff + a:d_ff + a + fc])
        h = (gate * jax.nn.sigmoid(gate) * up).astype(BF16)
        acc = acc + _dot(h, wdn_ref[a:a + fc, :])
    o_ref[...] = _layer_norm(alpha * x + 0.5 * acc, g_ref[...], b_ref[...])


def _ffn_ln(x, w_up, w_down, g, b, alpha, tm):
    n, d = x.shape
    d_ff = w_down.shape[0]
    n_chunk = 2 if d_ff % (2 * LANES) == 0 else 1
    return pl.pallas_call(
        functools.partial(_ffn_ln_kernel, alpha=alpha, d_ff=d_ff, n_chunk=n_chunk),
        grid=(n // tm,),
        in_specs=[pl.BlockSpec((tm, d), lambda i: (i, 0)),
                  _resident((d, 2 * d_ff)), _resident((d_ff, d)),
                  _resident((1, d)), _resident((1, d))],
        out_specs=pl.BlockSpec((tm, d), lambda i: (i, 0)),
        out_shape=jax.ShapeDtypeStruct((n, d), F32),
        compiler_params=_cparams(("parallel",)),
        name="ffn_ln",
    )(x, w_up, w_down, g, b)


def _in_proj_kernel(x_ref, w_ref, b_ref, ksb_ref, vsb_ref, kmb_ref, vmb_ref,
                    sb16_ref, mb16_ref, qmb_ref, u_ref, *, d_sb, d_mb, d_ssm):
    r = _dot(x_ref[...].astype(BF16), w_ref[...]) + b_ref[...]
    o = 0
    q_sb = r[:, o:o + d_sb]; o += d_sb
    k_sb = r[:, o:o + d_sb]; o += d_sb
    v_sb = r[:, o:o + d_sb]; o += d_sb
    q_mb = r[:, o:o + d_mb]; o += d_mb
    k_mb = r[:, o:o + d_mb]; o += d_mb
    v_mb = r[:, o:o + d_mb]; o += d_mb
    u = r[:, o:o + d_ssm]
    scale = HEAD_DIM ** -0.5
    ksb_ref[...] = k_sb
    vsb_ref[...] = v_sb
    kmb_ref[...] = k_mb
    vmb_ref[...] = v_mb
    sb16_ref[:, 0:d_sb] = (q_sb * scale).astype(BF16)
    sb16_ref[:, d_sb:2 * d_sb] = k_sb.astype(BF16)
    sb16_ref[:, 2 * d_sb:3 * d_sb] = v_sb.astype(BF16)
    mb16_ref[:, 0:d_mb] = (q_mb * scale).astype(BF16)
    mb16_ref[:, d_mb:2 * d_mb] = k_mb.astype(BF16)
    mb16_ref[:, 2 * d_mb:3 * d_mb] = v_mb.astype(BF16)
    qmb_ref[...] = q_mb
    u_ref[...] = u


def _in_proj(x, w, b, d_sb, d_mb, d_ssm, tm):
    n, d = x.shape
    d_out = w.shape[1]
    row = lambda width: pl.BlockSpec((tm, width), lambda i: (i, 0))
    widths = (d_sb, d_sb, d_mb, d_mb, 3 * d_sb, 3 * d_mb, d_mb, d_ssm)
    dtypes = (F32, F32, F32, F32, BF16, BF16, F32, F32)
    out_specs = [row(wd) for wd in widths]
    out_shape = [jax.ShapeDtypeStruct((n, wd), dt) for wd, dt in zip(widths, dtypes)]
    return pl.pallas_call(
        functools.partial(_in_proj_kernel, d_sb=d_sb, d_mb=d_mb, d_ssm=d_ssm),
        grid=(n // tm,),
        in_specs=[row(d), _resident((d, d_out)), _resident((1, d_out))],
        out_specs=out_specs,
        out_shape=out_shape,
        compiler_params=_cparams(("parallel",)),
        name="in_proj",
    )(x, w, b)


def _in_proj_prompt_kernel(x_ref, w_ref, b_ref, *refs, d_sb, d_mb, d_ssm, n_alias):
    (ksbt_ref, vsbt_ref, kmbt_ref, vmbt_ref, sb16_ref, kmb16_ref, vmbt16_ref,
     qmb_ref, u_ref, ksum_ref) = refs[n_alias:]
    r = _dot(x_ref[...].astype(BF16), w_ref[...]) + b_ref[...]
    o = 0
    q_sb = r[:, o:o + d_sb]; o += d_sb
    k_sb = r[:, o:o + d_sb]; o += d_sb
    v_sb = r[:, o:o + d_sb]; o += d_sb
    q_mb = r[:, o:o + d_mb]; o += d_mb
    k_mb = r[:, o:o + d_mb]; o += d_mb
    v_mb = r[:, o:o + d_mb]; o += d_mb
    u = r[:, o:o + d_ssm]
    scale = HEAD_DIM ** -0.5
    ksbt_ref[...] = k_sb.T
    vsbt_ref[...] = v_sb.T
    kmbt_ref[...] = k_mb.T
    v_mb_t = v_mb.T
    vmbt_ref[...] = v_mb_t
    vmbt16_ref[...] = v_mb_t.astype(BF16)
    sb16_ref[:, 0:d_sb] = (q_sb * scale).astype(BF16)
    sb16_ref[:, d_sb:2 * d_sb] = k_sb.astype(BF16)
    sb16_ref[:, 2 * d_sb:3 * d_sb] = v_sb.astype(BF16)
    kmb16_ref[...] = k_mb.astype(BF16)
    qmb_ref[...] = q_mb
    u_ref[...] = u
    tm = k_mb.shape[0]
    for j in range(tm // MOBA_BLOCK):
        ksum_ref[0, j:j + 1, :] = jnp.sum(
            k_mb[j * MOBA_BLOCK:(j + 1) * MOBA_BLOCK, :], axis=0, keepdims=True)


def _in_proj_prompt(x, w, b, layer, depth, bsz, t, d_sb, d_mb, d_ssm, tm, prev):
    n, d = x.shape
    d_out = w.shape[1]
    tpb = t // tm
    row = lambda width: pl.BlockSpec((tm, width), lambda i: (i, 0))
    kt = lambda width: pl.BlockSpec((None, None, width, tm), lambda i: (layer, i // tpb, 0, i % tpb))
    kt_shape = lambda width: jax.ShapeDtypeStruct((depth, bsz, width, t), F32)
    nblk = tm // MOBA_BLOCK
    out_specs = [kt(d_sb), kt(d_sb), kt(d_mb), kt(d_mb), row(3 * d_sb), row(d_mb),
                 pl.BlockSpec((None, d_mb, tm), lambda i: (i // tpb, 0, i % tpb)),
                 row(d_mb), row(d_ssm), pl.BlockSpec((1, nblk, d_mb), lambda i: (i, 0, 0))]
    out_shape = [kt_shape(d_sb), kt_shape(d_sb), kt_shape(d_mb), kt_shape(d_mb),
                 jax.ShapeDtypeStruct((n, 3 * d_sb), BF16), jax.ShapeDtypeStruct((n, d_mb), BF16),
                 jax.ShapeDtypeStruct((bsz, d_mb, t), BF16),
                 jax.ShapeDtypeStruct((n, d_mb), F32), jax.ShapeDtypeStruct((n, d_ssm), F32),
                 jax.ShapeDtypeStruct((n // tm, nblk, d_mb), F32)]
    in_specs = [row(d), _resident((d, d_out)), _resident((1, d_out))]
    args = [x, w, b]
    aliases = {}
    if prev is not None:
        in_specs += [pl.BlockSpec(memory_space=pl.ANY)] * len(prev)
        aliases = {len(args) + k: k for k in range(len(prev))}
        args += list(prev)
    return pl.pallas_call(
        functools.partial(_in_proj_prompt_kernel, d_sb=d_sb, d_mb=d_mb, d_ssm=d_ssm,
                          n_alias=len(aliases)),
        grid=(n // tm,),
        in_specs=in_specs,
        out_specs=out_specs,
        out_shape=out_shape,
        input_output_aliases=aliases,
        compiler_params=_cparams(("parallel",)),
        name="in_proj_prompt",
    )(*args)


def _sb_attn_kernel(q_ref, k_ref, v_ref, o_ref, *, tile):
    qi = pl.program_id(2)
    q = q_ref[...]
    lane_head = lax.broadcasted_iota(jnp.int32, (tile, LANES), 1) // HEAD_DIM
    row = lax.broadcasted_iota(jnp.int32, (tile, tile), 0)
    col = lax.broadcasted_iota(jnp.int32, (tile, tile), 1)
    tri = (row >= col).astype(BF16)

    heads = range(HEADS_PER_VREG)
    qhs = [jnp.where(lane_head == h, q, jnp.zeros_like(q)) for h in heads]

    def cond(st):
        kt, carries, _ = st
        top = functools.reduce(jnp.maximum, carries)
        return jnp.logical_and(kt >= 0, jnp.max(top) > F32_EXP_ZERO)

    def body(st):
        kt, carries, accs = st
        start = pl.multiple_of(kt * tile, tile)
        k = k_ref[pl.ds(start, tile), :]
        v = v_ref[pl.ds(start, tile), :]
        valid = jnp.logical_or(kt < qi, col < row)
        zs = [_dot_nt(qhs[h], k) for h in heads]
        lks = [jnp.where(valid, _neg_softplus(zs[h]), 0.0) for h in heads]
        parts = [_split_bf16(lks[h]) for h in heads]
        rcss = [_dot(parts[h][0], tri) + _dot(parts[h][1], tri) for h in heads]
        ws = [jnp.where(valid, jnp.exp(zs[h] + carries[h] + rcss[h]), 0.0) for h in heads]
        accs = [accs[h] + _dot(ws[h].astype(BF16), v) for h in heads]
        return kt - 1, [carries[h] + rcss[h][:, 0:1] for h in heads], accs

    init = (qi, [jnp.zeros((tile, 1), F32) for _ in heads],
            [jnp.zeros((tile, LANES), F32) for _ in heads])
    accs = lax.while_loop(cond, body, init)[2]
    out = jnp.where(lane_head == 0, accs[0], accs[1])
    o_ref[...] = out.astype(o_ref.dtype)


def _sb_attn(sb16, bsz, t, n_heads):
    tile = min(256, t)
    d_sb = n_heads * HEAD_DIM
    nlb = d_sb // LANES
    x = sb16.reshape(bsz, t, 3 * d_sb)
    return pl.pallas_call(
        functools.partial(_sb_attn_kernel, tile=tile),
        grid=(bsz, nlb, t // tile),
        in_specs=[pl.BlockSpec((None, tile, LANES), lambda b, p, i: (b, i, p)),
                  pl.BlockSpec((None, t, LANES), lambda b, p, i: (b, 0, nlb + p)),
                  pl.BlockSpec((None, t, LANES), lambda b, p, i: (b, 0, 2 * nlb + p))],
        out_specs=pl.BlockSpec((None, tile, LANES), lambda b, p, i: (b, i, p)),
        out_shape=jax.ShapeDtypeStruct((bsz, t, d_sb), BF16),
        compiler_params=_cparams(("parallel", "parallel", "arbitrary")),
        name="sb_attn",
    )(x, x, x)


def _moba_select_bias(score, blk, n_blocks):
    sel = jnp.zeros(score.shape, jnp.bool_)
    s = score
    for _ in range(min(MOBA_TOPK, n_blocks)):
        m = jnp.max(s, axis=0, keepdims=True)
        idx = jnp.min(jnp.where(s == m, blk, n_blocks), axis=0, keepdims=True)
        pick = blk == idx
        sel = jnp.logical_or(sel, jnp.logical_and(pick, jnp.abs(m) < jnp.inf))
        s = jnp.where(pick, -jnp.inf, s)
    return jnp.where(sel, 0.0, NEG_BIG)


def _moba_attn_kernel(q32_ref, k_ref, vt_ref, ksum_ref, o_ref, bias_ref,
                      za_ref, zb_ref, pa_ref, pb_ref, *, n_blocks):
    qi = pl.program_id(2)
    tile = MOBA_BLOCK
    step = 2 * tile
    qt = q32_ref[...].T
    row_head = lax.broadcasted_iota(jnp.int32, (LANES, tile), 0) // HEAD_DIM
    blk = lax.broadcasted_iota(jnp.int32, (n_blocks, tile), 0)
    ksum = ksum_ref[...]
    scale = HEAD_DIM ** -0.5 * LOG2_E

    heads = range(HEADS_PER_VREG)
    last_pair = n_blocks // 2 - 1
    qhs = [jnp.where(row_head == h, qt, 0.0) for h in heads]
    qts = [(qhs[h] * scale).astype(BF16) for h in heads]

    def put_logits(z_ref, j):
        start = pl.multiple_of(jnp.minimum(j, last_pair) * step, step)
        k2 = k_ref[pl.ds(start, step), :]
        for h in heads:
            z_ref[h] = _dot(k2, qts[h])

    put_logits(za_ref, 0)
    start = pl.multiple_of(qi * tile, tile)
    k_own = k_ref[pl.ds(start, tile), :]
    z_own = [_dot(k_own, qts[h]) for h in heads]

    for h in heads:
        score = jnp.dot(ksum, qhs[h], precision=_HI, preferred_element_type=F32) * (1.0 / MOBA_BLOCK)
        score = jnp.where(blk < qi, score, -jnp.inf)
        bias = _moba_select_bias(score, blk, n_blocks)
        for jp in range(n_blocks // 2):
            bias_ref[h, jp, 0:2, :] = bias[2 * jp:2 * jp + 2]

    key = lax.broadcasted_iota(jnp.int32, (tile, tile), 0)
    qry = lax.broadcasted_iota(jnp.int32, (tile, tile), 1)
    ms, ls, accs = [], [], []
    for h in heads:
        zt = jnp.where(key <= qry, z_own[h], NEG_BIG)
        m = jnp.max(zt, axis=0, keepdims=True)
        p = jnp.exp2(zt - m)
        ms.append(m)
        ls.append(jnp.sum(p, axis=0, keepdims=True))
        accs.append(_dot(vt_ref[h * HEAD_DIM:(h + 1) * HEAD_DIM, pl.ds(start, tile)], p.astype(BF16)))

    def weighted_values(p_ref, j):
        start = pl.multiple_of(jnp.clip(j, 0, last_pair) * step, step)
        return [_dot(vt_ref[h * HEAD_DIM:(h + 1) * HEAD_DIM, pl.ds(start, step)], p_ref[h])
                for h in heads]

    def trip(j, st, z_cur, z_next, p_cur, p_prev):
        ms, ls, accs, corrs = st
        put_logits(z_next, j + 1)
        pvs = weighted_values(p_prev, j - 1)
        out_m, out_l, out_acc, out_corr = [], [], [], []
        for h in heads:
            bias = bias_ref[h, jnp.minimum(j, last_pair), 0:2, :]
            halves = (z_cur[h, :tile], z_cur[h, tile:])
            tops = [jnp.max(halves[r], axis=0, keepdims=True) + bias[r:r + 1] for r in range(2)]
            m_new = jnp.maximum(ms[h], jnp.maximum(tops[0], tops[1]))
            p = jnp.concatenate([jnp.exp2(halves[r] - (m_new - bias[r:r + 1])) for r in range(2)],
                                axis=0)
            p_cur[h] = p.astype(BF16)
            corr = jnp.exp2(ms[h] - m_new)
            out_m.append(m_new)
            out_l.append(ls[h] * corr + jnp.sum(p, axis=0, keepdims=True))
            out_acc.append(accs[h] * corrs[h] + pvs[h])
            out_corr.append(corr)
        return out_m, out_l, out_acc, out_corr

    def two_trips(i, st):
        st = trip(2 * i, st, za_ref, zb_ref, pa_ref, pb_ref)
        return trip(2 * i + 1, st, zb_ref, za_ref, pb_ref, pa_ref)

    pb_ref[...] = jnp.zeros_like(pb_ref)
    ones = [jnp.ones((1, tile), F32) for _ in heads]
    n_trips = (qi + 1) // 2
    n_pairs = n_trips // 2
    st = lax.fori_loop(0, n_pairs, two_trips, (ms, ls, accs, ones))

    def drain(st, p_last, j_last):
        _, ls, accs, corrs = st
        pvs = weighted_values(p_last, j_last)
        return [accs[h] * corrs[h] + pvs[h] for h in heads], ls

    def odd_tail(st):
        return drain(trip(2 * n_pairs, st, za_ref, zb_ref, pa_ref, pb_ref), pa_ref, 2 * n_pairs)

    def even_tail(st):
        return drain(st, pb_ref, 2 * n_pairs - 1)

    accs, ls = lax.cond(n_trips % 2 == 1, odd_tail, even_tail, st)
    out_t = jnp.concatenate([accs[h] / ls[h] for h in heads], axis=0)
    o_ref[...] = out_t.T.astype(o_ref.dtype)


def _moba_attn(k16, vt16, q_mb32, ksum, bsz, t, n_heads):
    tile = MOBA_BLOCK
    d_mb = n_heads * HEAD_DIM
    nlb = d_mb // LANES
    n_blocks = t // tile
    assert n_blocks % 2 == 0
    x = k16.reshape(bsz, t, d_mb)
    vt = vt16
    q32 = q_mb32.reshape(bsz, t, d_mb)
    return pl.pallas_call(
        functools.partial(_moba_attn_kernel, n_blocks=n_blocks),
        grid=(bsz, nlb, n_blocks),
        in_specs=[pl.BlockSpec((None, tile, LANES), lambda b, p, i: (b, i, p)),
                  pl.BlockSpec((None, t, LANES), lambda b, p, i: (b, 0, p)),
                  pl.BlockSpec((None, LANES, t), lambda b, p, i: (b, p, 0)),
                  pl.BlockSpec((None, n_blocks, LANES), lambda b, p, i: (b, 0, p))],
        out_specs=pl.BlockSpec((None, tile, LANES), lambda b, p, i: (b, i, p)),
        out_shape=jax.ShapeDtypeStruct((bsz, t, d_mb), BF16),
        scratch_shapes=[pltpu.VMEM((HEADS_PER_VREG, n_blocks // 2, SUBLANES, tile), F32)]
        + [pltpu.VMEM((HEADS_PER_VREG, 2 * tile, tile), F32)] * 2
        + [pltpu.VMEM((HEADS_PER_VREG, 2 * tile, tile), BF16)] * 2,
        compiler_params=_cparams(("parallel", "parallel", "arbitrary")),
        name="moba_attn",
    )(q32, x, vt, ksum)


def _ssm_tables(a_re, a_im, log_dt, b_re, b_im, c_re, c_im):
    ng, ns = a_re.shape
    nc = SSM_GROUP_CH
    L = SSM_CHUNK
    a_re, a_im = a_re.astype(F32), a_im.astype(F32)
    dt = jnp.exp(log_dt.astype(F32))[:, None]
    tau = jnp.arange(L + 1, dtype=F32)[:, None, None]
    mag = jnp.exp(tau * (a_re * dt))
    ang = tau * (a_im * dt)
    pw_re, pw_im = mag * jnp.cos(ang), mag * jnp.sin(ang)
    lam_re, lam_im = pw_re[1], pw_im[1]
    den = a_re * a_re + a_im * a_im
    num_re, num_im = lam_re - 1.0, lam_im
    coef_re = (num_re * a_re + num_im * a_im) / den
    coef_im = (num_im * a_re - num_re * a_im) / den
    b_re, b_im = b_re.astype(F32), b_im.astype(F32)
    bb_re = coef_re[..., None] * b_re - coef_im[..., None] * b_im
    bb_im = coef_re[..., None] * b_im + coef_im[..., None] * b_re
    c_re, c_im = c_re.astype(F32), c_im.astype(F32)
    x_re = pw_re[..., None] * bb_re - pw_im[..., None] * bb_im
    x_im = pw_re[..., None] * bb_im + pw_im[..., None] * bb_re
    kern = (jnp.einsum('gcp,tgpd->tgcd', c_re, x_re, precision=_HI)
            - jnp.einsum('gcp,tgpd->tgcd', c_im, x_im, precision=_HI))
    eye = jnp.eye(ng, dtype=F32)
    blockdiag = lambda w: jnp.einsum('gab,gh->gahb', w, eye).reshape(ng * w.shape[1], ng * w.shape[2])
    k_lag = jnp.einsum('tgcd,gh->tgdhc', kern[:L], eye).reshape(L, ng * nc, ng * nc)
    return dict(lam_re=lam_re, lam_im=lam_im,
                k_lag=k_lag.astype(BF16),
                wb_re=blockdiag(jnp.swapaxes(bb_re, 1, 2)),
                wb_im=blockdiag(jnp.swapaxes(bb_im, 1, 2)),
                wc_re=blockdiag(jnp.swapaxes(c_re, 1, 2)),
                wc_im=blockdiag(jnp.swapaxes(c_im, 1, 2)),
                pw_re=pw_re.reshape(L + 1, ng * ns), pw_im=pw_im.reshape(L + 1, ng * ns))


def _ssm_rows_kernel(u_ref, klag_ref, wbr_ref, wbi_ref, wcr_ref, wci_ref, pwr_ref, pwi_ref, d_ref,
                     y_ref, hre_ref, him_ref, sre_ref, sim_ref, pre_ref, pim_ref, cre_ref, cim_ref,
                     *half_refs, n_chunks):
    L = SSM_CHUNK
    i = pl.program_id(1)

    @pl.when(i == 0)
    def _():
        cre_ref[...] = jnp.zeros_like(cre_ref)
        cim_ref[...] = jnp.zeros_like(cim_ref)

    n_half = len(half_refs) // 2
    u_half, y_half = half_refs[:n_half], half_refs[n_half:]
    for k in range(n_half):
        u_half[k][...] = u_ref[:, k * LANES:(k + 1) * LANES]
    us = [jnp.concatenate([u_half[k][pl.ds(s, n_chunks, stride=L), :] for k in range(n_half)],
                          axis=1) for s in range(L)]
    ub = [x.astype(BF16) for x in us]

    s_re = s_im = None
    for s in range(L):
        br, bi = _dot(ub[s], wbr_ref[...]), _dot(ub[s], wbi_ref[...])
        lr, li = pwr_ref[L - 1 - s:L - s, :], pwi_ref[L - 1 - s:L - s, :]
        tr, ti = lr * br - li * bi, lr * bi + li * br
        s_re, s_im = (tr, ti) if s_re is None else (s_re + tr, s_im + ti)
    sre_ref[...] = s_re
    sim_ref[...] = s_im

    lr, li = pwr_ref[L:L + 1, :], pwi_ref[L:L + 1, :]

    def chunk(c, st):
        hr, hi = st
        pre_ref[pl.ds(c, 1), :] = hr
        pim_ref[pl.ds(c, 1), :] = hi
        return (lr * hr - li * hi + sre_ref[pl.ds(c, 1), :],
                lr * hi + li * hr + sim_ref[pl.ds(c, 1), :])

    hr, hi = lax.fori_loop(0, n_chunks, chunk, (cre_ref[...], cim_ref[...]))
    cre_ref[...] = hr
    cim_ref[...] = hi
    hre_ref[...] = hr
    him_ref[...] = hi

    lr, li = pwr_ref[1:2, :], pwi_ref[1:2, :]
    g_re, g_im = pre_ref[...], pim_ref[...]
    for t in range(L):
        g_re, g_im = lr * g_re - li * g_im, lr * g_im + li * g_re
        y = (_dot(g_re.astype(BF16), wcr_ref[...]) - _dot(g_im.astype(BF16), wci_ref[...])
             + d_ref[...] * us[t])
        for s in range(t + 1):
            y = y + _dot(ub[s], klag_ref[t - s])
        for k in range(n_half):
            y_half[k][pl.ds(t, n_chunks, stride=L), :] = y[:, k * LANES:(k + 1) * LANES]
    for k in range(n_half):
        y_ref[:, k * LANES:(k + 1) * LANES] = y_half[k][...]


def _ssm_prompt(u, tabs, ssm_d, bsz, t, n_groups, n_state):
    L = SSM_CHUNK
    n, dc = u.shape
    ds = n_groups * n_state
    rows = min(t, SSM_ROWS)
    assert t % rows == 0 and rows % (L * SUBLANES) == 0
    n_chunks = rows // L
    tiles = t // rows
    bf = lambda a: a.astype(BF16)
    state = pl.BlockSpec((None, 1, ds), lambda b, i: (b, 0, 0))
    y, h_re, h_im = pl.pallas_call(
        functools.partial(_ssm_rows_kernel, n_chunks=n_chunks),
        grid=(bsz, tiles),
        in_specs=[pl.BlockSpec((rows, dc), lambda b, i: (b * tiles + i, 0)),
                  _resident((L, dc, dc)), _resident((dc, ds)), _resident((dc, ds)),
                  _resident((ds, dc)), _resident((ds, dc)),
                  _resident((L + 1, ds)), _resident((L + 1, ds)), _resident((1, dc))],
        out_specs=[pl.BlockSpec((rows, dc), lambda b, i: (b * tiles + i, 0)), state, state],
        out_shape=[jax.ShapeDtypeStruct((n, dc), F32),
                   jax.ShapeDtypeStruct((bsz, 1, ds), F32), jax.ShapeDtypeStruct((bsz, 1, ds), F32)],
        scratch_shapes=[pltpu.VMEM((n_chunks, ds), F32)] * 4 + [pltpu.VMEM((1, ds), F32)] * 2
        + [pltpu.VMEM((rows, LANES), F32)] * (2 * (dc // LANES)),
        compiler_params=_cparams(("parallel", "arbitrary")),
        name="ssm_rows",
    )(u, tabs['k_lag'], bf(tabs['wb_re']), bf(tabs['wb_im']), bf(tabs['wc_re']), bf(tabs['wc_im']),
      tabs['pw_re'], tabs['pw_im'], ssm_d.astype(F32).reshape(1, dc))
    return y, h_re.reshape(bsz, n_groups, n_state), h_im.reshape(bsz, n_groups, n_state)


def _ssm_step_kernel(u_ref, hr_ref, hi_ref, wbr_ref, wbi_ref, lr_ref, li_ref,
                     wcr_ref, wci_ref, d_ref, y_ref, hro_ref, hio_ref):
    u = u_ref[...]
    dot = lambda a, b: jnp.dot(a, b, precision=_HI, preferred_element_type=F32)
    h0r, h0i = hr_ref[...], hi_ref[...]
    lr, li = lr_ref[...], li_ref[...]
    hr = dot(u, wbr_ref[...]) + (lr * h0r - li * h0i)
    hi = dot(u, wbi_ref[...]) + (lr * h0i + li * h0r)
    hro_ref[...] = hr
    hio_ref[...] = hi
    y_ref[...] = dot(hr, wcr_ref[...]) - dot(hi, wci_ref[...]) + d_ref[...] * u


def _ssm_step(u, h0_re, h0_im, tabs, ssm_d):
    bsz, ng, ns = h0_re.shape
    nc = SSM_GROUP_CH
    wb_re, wb_im, wc_re, wc_im = tabs['wb_re'], tabs['wb_im'], tabs['wc_re'], tabs['wc_im']
    flat = lambda a: a.astype(F32).reshape(1, ng * ns)
    y, hr, hi = pl.pallas_call(
        _ssm_step_kernel,
        out_shape=[jax.ShapeDtypeStruct((bsz, ng * nc), F32),
                   jax.ShapeDtypeStruct((bsz, ng * ns), F32),
                   jax.ShapeDtypeStruct((bsz, ng * ns), F32)],
        compiler_params=pltpu.CompilerParams(vmem_limit_bytes=VMEM_LIMIT),
        name="ssm_step",
    )(u, h0_re.astype(F32).reshape(bsz, ng * ns), h0_im.astype(F32).reshape(bsz, ng * ns),
      wb_re, wb_im, flat(tabs['lam_re']), flat(tabs['lam_im']), wc_re, wc_im,
      ssm_d.astype(F32).reshape(1, ng * nc))
    return y, hr.reshape(bsz, ng, ns), hi.reshape(bsz, ng, ns)


def _merge_kernel(x_ref, osb_ref, omb_ref, yssm_ref, wg_ref, bg_ref, wglu_ref,
                  wbsb_ref, wbmb_ref, wbssm_ref, wout_ref, g_ref, b_ref, o_ref, *, alpha):
    x = x_ref[...]
    d = x.shape[1]
    gates = jax.nn.sigmoid(_dot(x.astype(BF16), wg_ref[...]) + bg_ref[...])
    glu = _dot(yssm_ref[...].astype(BF16), wglu_ref[...])
    half = glu.shape[1] // 2
    o_ssm = glu[:, :half] * jax.nn.sigmoid(glu[:, half:])
    merged = (gates[:, 0:d] * _dot(osb_ref[...], wbsb_ref[...])
              + gates[:, d:2 * d] * _dot(omb_ref[...], wbmb_ref[...])
              + gates[:, 2 * d:3 * d] * _dot(o_ssm.astype(BF16), wbssm_ref[...]))
    mix = _dot(merged.astype(BF16), wout_ref[...])
    o_ref[...] = _layer_norm(alpha * x + mix, g_ref[...], b_ref[...])


def _merge(x, o_sb, o_mb, y_ssm, lw, alpha, tm):
    n, d = x.shape
    row = lambda a: pl.BlockSpec((tm, a.shape[1]), lambda i: (i, 0))
    acts = (x, o_sb, o_mb, y_ssm)
    weights = (lw['w_gate'], lw['b_gate'], lw['w_glu'], lw['w_br_sb'], lw['w_br_mb'],
               lw['w_br_ssm'], lw['w_out'], lw['ln2_g'], lw['ln2_b'])
    return pl.pallas_call(
        functools.partial(_merge_kernel, alpha=alpha),
        grid=(n // tm,),
        in_specs=[row(a) for a in acts] + [_resident(w.shape) for w in weights],
        out_specs=pl.BlockSpec((tm, d), lambda i: (i, 0)),
        out_shape=jax.ShapeDtypeStruct((n, d), F32),
        compiler_params=_cparams(("parallel",)),
        name="merge",
    )(*acts, *weights)


def _pages_keys_minor(cache):
    return jnp.transpose(cache, (0, 1, 3, 4, 2))


def _sb_decode_kernel(pt_ref, q_ref, ck_ref, cv_ref, o_ref, kbuf, vbuf, sem,
                      *, n_heads, n_pages, layer):
    b = pl.program_id(0)
    page = kbuf.shape[-1]
    d = n_heads * HEAD_DIM
    assert n_heads <= SUBLANES

    def page_copies(j, slot):
        pg = pt_ref[b, n_pages - 1 - j]
        return (pltpu.make_async_copy(ck_ref.at[layer, pg], kbuf.at[slot], sem.at[slot, 0]),
                pltpu.make_async_copy(cv_ref.at[layer, pg], vbuf.at[slot], sem.at[slot, 1]))

    def start_page(j, slot):
        for cp in page_copies(j, slot):
            cp.start()

    def wait_page(j, slot):
        for cp in page_copies(j, slot):
            cp.wait()

    q = q_ref[...] * (HEAD_DIM ** -0.5)
    row = lax.broadcasted_iota(jnp.int32, (page, page), 0)
    col = lax.broadcasted_iota(jnp.int32, (page, page), 1)
    tri = (row >= col).astype(BF16)
    head_rows = lax.broadcasted_iota(jnp.int32, (SUBLANES, 1), 0) < n_heads
    zero_rows = jnp.zeros((SUBLANES - n_heads, page), F32)

    def alive(carry):
        return jnp.max(jnp.where(head_rows, carry, -jnp.inf)) > F32_EXP_ZERO

    def cond(st):
        j, carry, _ = st
        return jnp.logical_and(j < n_pages, alive(carry))

    def body(st):
        j, carry, acc = st
        slot = j % 2

        @pl.when(j + 1 < n_pages)
        def _():
            start_page(j + 1, 1 - slot)

        wait_page(j, slot)
        z = jnp.concatenate(
            [jnp.sum(kbuf[slot, h] * q[h * HEAD_DIM:(h + 1) * HEAD_DIM], axis=0, keepdims=True)
             for h in range(n_heads)] + [zero_rows], axis=0)
        lk = _neg_softplus(z)
        hi, lo = _split_bf16(lk)
        rcs = _dot(hi, tri) + _dot(lo, tri)
        w = jnp.exp(z + carry + rcs)
        acc = acc + jnp.concatenate(
            [jnp.sum(vbuf[slot, h] * w[h:h + 1, :], axis=-1, keepdims=True)
             for h in range(n_heads)], axis=0)
        return j + 1, carry + rcs[:, 0:1], acc

    start_page(0, 0)
    j, _, acc = lax.while_loop(
        cond, body, (jnp.int32(0), jnp.zeros((SUBLANES, 1), F32), jnp.zeros((d, 1), F32)))

    @pl.when(j < n_pages)
    def _():
        wait_page(j, j % 2)

    o_ref[...] = acc


def _sb_decode(q, cache_k, cache_v, layer, page_table, n_heads):
    bsz, n_pages = page_table.shape
    page = cache_k.shape[2]
    d = n_heads * HEAD_DIM
    ck, cv = _pages_keys_minor(cache_k), _pages_keys_minor(cache_v)
    col = pl.BlockSpec((None, d, 1), lambda b, pt: (b, 0, 0))
    grid_spec = pltpu.PrefetchScalarGridSpec(
        num_scalar_prefetch=1,
        grid=(bsz,),
        in_specs=[col, pl.BlockSpec(memory_space=pl.ANY), pl.BlockSpec(memory_space=pl.ANY)],
        out_specs=col,
        scratch_shapes=[pltpu.VMEM((2, n_heads, HEAD_DIM, page), F32),
                        pltpu.VMEM((2, n_heads, HEAD_DIM, page), F32),
                        pltpu.SemaphoreType.DMA((2, 2))],
    )
    out = pl.pallas_call(
        functools.partial(_sb_decode_kernel, n_heads=n_heads, n_pages=n_pages, layer=layer),
        grid_spec=grid_spec,
        out_shape=jax.ShapeDtypeStruct((bsz, d, 1), F32),
        compiler_params=_cparams(("arbitrary",)),
        name="sb_decode",
    )(page_table, q.reshape(bsz, d, 1), ck, cv)
    return out.reshape(bsz, d).astype(BF16)


def _moba_ksum_kernel(pt_ref, ck_ref, o_ref, buf, sem, *, n_pages, ppb, layer):
    ring = buf.shape[0]
    b = pl.program_id(0)
    total = pl.num_programs(0) * n_pages

    def page_copy(g, slot):
        pg = pt_ref[g // n_pages, g % n_pages]
        return pltpu.make_async_copy(ck_ref.at[layer, pg], buf.at[slot], sem.at[slot])

    @pl.when(b == 0)
    def _():
        for r in range(ring):
            page_copy(r, r).start()

    lane = lax.broadcasted_iota(jnp.int32, o_ref.shape, 1)

    def block(blk, out):
        s = None
        for r in range(ppb):
            g = b * n_pages + blk * ppb + r
            slot = g % ring
            page_copy(g, slot).wait()
            rows = buf[slot]
            s = rows if s is None else s + rows

            @pl.when(g + ring < total)
            def _():
                page_copy(g + ring, slot).start()

        col = jnp.sum(s.reshape(-1, s.shape[-1]), axis=-1, keepdims=True)
        return jnp.where(lane == blk, col, out)

    o_ref[...] = lax.fori_loop(0, n_pages // ppb, block, jnp.zeros(o_ref.shape, F32))


def _moba_ksum(cache_k, layer, page_table, n_heads):
    bsz, n_pages = page_table.shape
    page = cache_k.shape[2]
    d = n_heads * HEAD_DIM
    ppb = MOBA_BLOCK // page
    n_blocks = n_pages // ppb
    ring = min(8, bsz * n_pages)
    ck = _pages_keys_minor(cache_k)
    grid_spec = pltpu.PrefetchScalarGridSpec(
        num_scalar_prefetch=1,
        grid=(bsz,),
        in_specs=[pl.BlockSpec(memory_space=pl.ANY)],
        out_specs=pl.BlockSpec((None, d, n_blocks), lambda b, pt: (b, 0, 0)),
        scratch_shapes=[pltpu.VMEM((ring, n_heads, HEAD_DIM, page), F32),
                        pltpu.SemaphoreType.DMA((ring,))],
    )
    return pl.pallas_call(
        functools.partial(_moba_ksum_kernel, n_pages=n_pages, ppb=ppb, layer=layer),
        grid_spec=grid_spec,
        out_shape=jax.ShapeDtypeStruct((bsz, d, n_blocks), F32),
        compiler_params=_cparams(("arbitrary",)),
        name="moba_ksum",
    )(page_table, ck)


def _moba_topk_kernel(q_ref, ksum_ref, idx_ref, ok_ref, *, n_heads, n_blocks):
    prod = ksum_ref[...] * q_ref[...]
    score = jnp.sum(prod.reshape(n_heads, HEAD_DIM, n_blocks), axis=1) * (1.0 / MOBA_BLOCK)
    blk = lax.broadcasted_iota(jnp.int32, (n_heads, n_blocks), 1)
    lane = lax.broadcasted_iota(jnp.int32, (n_heads, LANES), 1)
    s = score
    idx_out = jnp.zeros((n_heads, LANES), jnp.int32)
    ok_out = jnp.zeros((n_heads, LANES), jnp.int32)
    for r in range(MOBA_TOPK):
        m = jnp.max(s, axis=-1, keepdims=True)
        idx = jnp.min(jnp.where(s == m, blk, n_blocks), axis=-1, keepdims=True)
        idx_out = jnp.where(lane == r, jnp.minimum(idx, n_blocks - 1), idx_out)
        ok_out = jnp.where(lane == r, (jnp.abs(m) < jnp.inf).astype(jnp.int32), ok_out)
        s = jnp.where(blk == idx, -jnp.inf, s)
    idx_ref[...] = idx_out
    ok_ref[...] = ok_out


def _moba_topk(q, ksum_t, n_heads):
    bsz, d, n_blocks = ksum_t.shape
    assert n_blocks >= MOBA_TOPK
    idx, ok = pl.pallas_call(
        functools.partial(_moba_topk_kernel, n_heads=n_heads, n_blocks=n_blocks),
        grid=(bsz,),
        in_specs=[pl.BlockSpec((None, d, 1), lambda b: (b, 0, 0)),
                  pl.BlockSpec((None, d, n_blocks), lambda b: (b, 0, 0))],
        out_specs=[pl.BlockSpec((None, n_heads, LANES), lambda b: (b, 0, 0))] * 2,
        out_shape=[jax.ShapeDtypeStruct((bsz, n_heads, LANES), jnp.int32)] * 2,
        compiler_params=_cparams(("parallel",)),
        name="moba_topk",
    )(q.reshape(bsz, d, 1), ksum_t)
    return idx[:, :, :MOBA_TOPK], ok[:, :, :MOBA_TOPK]


def _moba_decode_kernel(pt_ref, idx_ref, ok_ref, q_ref, kn_ref, vn_ref, *refs, n_tiles):
    k_refs = refs[:n_tiles]
    v_refs = refs[n_tiles:2 * n_tiles]
    o_ref = refs[2 * n_tiles]
    b = pl.program_id(0)
    h = pl.program_id(1)
    n_heads = pl.num_programs(1)
    ppb = n_tiles // MOBA_TOPK
    q = q_ref[...] * (HEAD_DIM ** -0.5)
    z_own = jnp.sum(q * kn_ref[...], axis=0, keepdims=True)
    zs = []
    for t in range(n_tiles):
        ok = ok_ref[(b * n_heads + h) * MOBA_TOPK + t // ppb] > 0
        z = jnp.sum(k_refs[t][...] * q, axis=0, keepdims=True)
        zs.append(jnp.where(ok, z, NEG_BIG))
    m = z_own
    for z in zs:
        m = jnp.maximum(m, jnp.max(z, axis=-1, keepdims=True))
    p_own = jnp.exp(z_own - m)
    l = p_own
    acc = p_own * vn_ref[...]
    for t in range(n_tiles):
        pw = jnp.exp(zs[t] - m)
        l = l + jnp.sum(pw, axis=-1, keepdims=True)
        acc = acc + jnp.sum(v_refs[t][...] * pw, axis=-1, keepdims=True)
    o_ref[...] = acc / l


def _moba_decode(q, k_new, v_new, idx, ok, cache_k, cache_v, layer, page_table, n_heads):
    bsz, n_pages = page_table.shape
    page = cache_k.shape[2]
    d = n_heads * HEAD_DIM
    ppb = MOBA_BLOCK // page
    n_tiles = MOBA_TOPK * ppb
    ck, cv = _pages_keys_minor(cache_k), _pages_keys_minor(cache_v)

    def tile_spec(t):
        def index_map(b, h, pt, idx, ok):
            blk = idx[(b * n_heads + h) * MOBA_TOPK + t // ppb]
            return (layer, pt[b, blk * ppb + t % ppb], h, 0, 0)
        return pl.BlockSpec((None, None, None, HEAD_DIM, page), index_map)

    tiles = [tile_spec(t) for t in range(n_tiles)]
    col = pl.BlockSpec((None, HEAD_DIM, 1), lambda b, h, pt, idx, ok: (b, h, 0))
    grid_spec = pltpu.PrefetchScalarGridSpec(
        num_scalar_prefetch=3,
        grid=(bsz, n_heads),
        in_specs=[col, col, col] + tiles * 2,
        out_specs=col,
    )
    out = pl.pallas_call(
        functools.partial(_moba_decode_kernel, n_tiles=n_tiles),
        grid_spec=grid_spec,
        out_shape=jax.ShapeDtypeStruct((bsz, d, 1), F32),
        compiler_params=_cparams(("parallel", "parallel")),
        name="moba_decode",
    )(page_table, idx.reshape(-1), ok.reshape(-1), q.reshape(bsz, d, 1), k_new.reshape(bsz, d, 1),
      v_new.reshape(bsz, d, 1), *([ck] * n_tiles), *([cv] * n_tiles))
    return out.reshape(bsz, d).astype(BF16)


def _row_tile(n):
    for tm in (512, 256, 128, 64, 32, 16, 8):
        if n % tm == 0:
            return tm
    return n


def _layer_weights(w, l, d_model, d_sb, d_mb, d_ssm):
    n_qkvu = 3 * d_sb + 3 * d_mb + d_ssm
    bf = lambda a: a.astype(BF16)
    vec = lambda a: a.astype(F32).reshape(1, -1)
    lw = dict(
        w_ffn1_up=bf(w['w_ffn1_up'][l]), w_ffn1_down=bf(w['w_ffn1_down'][l]),
        w_ffn2_up=bf(w['w_ffn2_up'][l]), w_ffn2_down=bf(w['w_ffn2_down'][l]),
        w_qkvu=bf(w['w_in'][l][:, :n_qkvu]), b_qkvu=vec(w['b_in'][l][:n_qkvu]),
        w_gate=bf(w['w_in'][l][:, n_qkvu:]), b_gate=vec(w['b_in'][l][n_qkvu:]),
        w_glu=bf(w['w_glu'][l]), w_br_sb=bf(w['w_br_sb'][l]), w_br_mb=bf(w['w_br_mb'][l]),
        w_br_ssm=bf(w['w_br_ssm'][l]), w_out=bf(w['w_out'][l]),
    )
    for name in ('ln1_g', 'ln1_b', 'ln2_g', 'ln2_b', 'ln3_g', 'ln3_b'):
        lw[name] = vec(w[name][l])
    return lw


def kernel(x_prompt, x_sample, cache_k_sb, cache_v_sb, cache_k_mb, cache_v_mb, state_ssm_re, state_ssm_im, page_table, ln1_g, ln1_b, w_ffn1_up, w_ffn1_down, w_in, b_in, ssm_a_re, ssm_a_im, ssm_log_dt, ssm_b_re, ssm_b_im, ssm_c_re, ssm_c_im, ssm_d, w_glu, w_br_sb, w_br_mb, w_br_ssm, w_out, ln2_g, ln2_b, w_ffn2_up, w_ffn2_down, ln3_g, ln3_b):
    weights = dict(ln1_g=ln1_g, ln1_b=ln1_b, w_ffn1_up=w_ffn1_up, w_ffn1_down=w_ffn1_down,
                   w_in=w_in, b_in=b_in, w_glu=w_glu, w_br_sb=w_br_sb, w_br_mb=w_br_mb,
                   w_br_ssm=w_br_ssm, w_out=w_out, ln2_g=ln2_g, ln2_b=ln2_b,
                   w_ffn2_up=w_ffn2_up, w_ffn2_down=w_ffn2_down, ln3_g=ln3_g, ln3_b=ln3_b)
    depth = w_in.shape[0]
    bsz, t, d_model = x_prompt.shape
    dbsz, dec_t, _ = x_sample.shape
    h_sb, h_mb = cache_k_sb.shape[3], cache_k_mb.shape[3]
    n_groups, n_state = ssm_a_re.shape[1:]
    d_sb, d_mb, d_ssm = h_sb * HEAD_DIM, h_mb * HEAD_DIM, n_groups * SSM_GROUP_CH
    page = cache_k_sb.shape[2]
    past_len = page_table.shape[1] * page
    assert dec_t == 1 and t % MOBA_BLOCK == 0 and t % SSM_CHUNK == 0
    assert MOBA_BLOCK % page == 0 and past_len % MOBA_BLOCK == 0
    assert d_sb % LANES == 0 and d_mb % LANES == 0
    alpha = float((2 * depth) ** 0.25)

    n_p = bsz * t
    tm_p = _row_tile(n_p)
    if tm_p % MOBA_BLOCK:
        tm_p = MOBA_BLOCK
    tm_s = _row_tile(dbsz)

    y_p = x_prompt.reshape(n_p, d_model)
    y_s = x_sample.reshape(dbsz, d_model)
    prompt_states, sample_states = [], []
    prompt_kv = None
    for l in range(depth):
        lw = _layer_weights(weights, l, d_model, d_sb, d_mb, d_ssm)
        tabs = _ssm_tables(ssm_a_re[l], ssm_a_im[l], ssm_log_dt[l], ssm_b_re[l], ssm_b_im[l],
                           ssm_c_re[l], ssm_c_im[l])

        x1 = _ffn_ln(y_p, lw['w_ffn1_up'], lw['w_ffn1_down'], lw['ln1_g'], lw['ln1_b'], alpha, tm_p)
        (*prompt_kv, sb16, kmb16, vmbt16, q_mb32, u, ksum) = _in_proj_prompt(
            x1, lw['w_qkvu'], lw['b_qkvu'], l, depth, bsz, t, d_sb, d_mb, d_ssm, tm_p, prompt_kv)
        o_sb = _sb_attn(sb16, bsz, t, h_sb).reshape(n_p, d_sb)
        o_mb = _moba_attn(kmb16, vmbt16, q_mb32, ksum.reshape(bsz, t // MOBA_BLOCK, d_mb),
                          bsz, t, h_mb).reshape(n_p, d_mb)
        y_ssm, h_re, h_im = _ssm_prompt(u, tabs, ssm_d[l], bsz, t, n_groups, n_state)
        x2 = _merge(x1, o_sb, o_mb, y_ssm, lw, alpha, tm_p)
        y_p = _ffn_ln(x2, lw['w_ffn2_up'], lw['w_ffn2_down'], lw['ln3_g'], lw['ln3_b'], alpha, tm_p)
        prompt_states.append((h_re.astype(state_ssm_re.dtype), h_im.astype(state_ssm_im.dtype)))

        s1 = _ffn_ln(y_s, lw['w_ffn1_up'], lw['w_ffn1_down'], lw['ln1_g'], lw['ln1_b'], alpha, tm_s)
        (k_sb, v_sb, k_mb, v_mb, sb16, mb16, q_mb32, u) = _in_proj(
            s1, lw['w_qkvu'], lw['b_qkvu'], d_sb, d_mb, d_ssm, tm_s)
        q_sb32 = sb16[:, :d_sb].astype(F32) * (HEAD_DIM ** 0.5)
        o_sb = _sb_decode(q_sb32, cache_k_sb, cache_v_sb, l, page_table, h_sb)
        ksum = _moba_ksum(cache_k_mb, l, page_table, h_mb)
        idx, ok = _moba_topk(q_mb32, ksum, h_mb)
        o_mb = _moba_decode(q_mb32, k_mb, v_mb, idx, ok, cache_k_mb, cache_v_mb, l, page_table, h_mb)
        y_ssm, h_re, h_im = _ssm_step(u, state_ssm_re[l], state_ssm_im[l], tabs, ssm_d[l])
        s2 = _merge(s1, o_sb, o_mb, y_ssm, lw, alpha, tm_s)
        y_s = _ffn_ln(s2, lw['w_ffn2_up'], lw['w_ffn2_down'], lw['ln3_g'], lw['ln3_b'], alpha, tm_s)
        sample_states.append((k_sb.reshape(dbsz, 1, h_sb, HEAD_DIM), v_sb.reshape(dbsz, 1, h_sb, HEAD_DIM),
                              k_mb.reshape(dbsz, 1, h_mb, HEAD_DIM), v_mb.reshape(dbsz, 1, h_mb, HEAD_DIM),
                              h_re.astype(state_ssm_re.dtype), h_im.astype(state_ssm_im.dtype)))

    stack = lambda states: tuple(jnp.stack(group) for group in zip(*states))
    h_re_p, h_im_p = stack(prompt_states)
    rows_major = lambda a, h: jnp.transpose(a.reshape(depth, bsz, h, HEAD_DIM, t), (0, 1, 4, 2, 3))
    k_sb_p, v_sb_p = rows_major(prompt_kv[0], h_sb), rows_major(prompt_kv[1], h_sb)
    k_mb_p, v_mb_p = rows_major(prompt_kv[2], h_mb), rows_major(prompt_kv[3], h_mb)
    k_sb_s, v_sb_s, k_mb_s, v_mb_s, h_re_s, h_im_s = stack(sample_states)
    return (y_p.reshape(bsz, t, d_model), y_s.reshape(dbsz, dec_t, d_model),
            k_sb_p, v_sb_p, k_mb_p, v_mb_p, h_re_p, h_im_p,
            k_sb_s, v_sb_s, k_mb_s, v_mb_s, h_re_s, h_im_s)
```
